```python
import jax, jax.numpy as jnp
from jax import lax
import numpy as np

D_MODEL = 1024
BATCH = 16
SEQ = 4096
DEPTH = 1

MLA_HEADS = 8
QK_NOPE_DIM = 128
QK_ROPE_DIM = 64
V_HEAD_DIM = 128
Q_LORA_RANK = 256
KV_LORA_RANK = 256
ROPE_THETA = 10000.0
Q_BLOCK = 128
SSM_D_INNER = 2 * D_MODEL
SSM_HEAD_DIM = 64
SSM_HEADS = SSM_D_INNER // SSM_HEAD_DIM
SSM_GROUPS = 8
SSM_HEADS_PER_GROUP = SSM_HEADS // SSM_GROUPS
SSM_STATE = 128
CONV_WIDTH = 4
SSD_CHUNK = 128
CONV_CH = SSM_D_INNER + 2 * SSM_GROUPS * SSM_STATE
D_FF = 4 * D_MODEL
EPS = 1e-6
IN_SIZES = (Q_LORA_RANK, KV_LORA_RANK, QK_ROPE_DIM, SSM_D_INNER, CONV_CH, SSM_HEADS, D_MODEL, D_MODEL)
IN_COLS = Q_LORA_RANK + KV_LORA_RANK + QK_ROPE_DIM + SSM_D_INNER + CONV_CH + SSM_HEADS + 2 * D_MODEL

kernel_name = "hybrid_mla_ssd_gated_block"


def rms_norm(x, g):
    xf = x.astype(jnp.float32)
    y = xf * lax.rsqrt(jnp.mean(xf * xf, axis=-1, keepdims=True) + EPS)
    return (y * g.astype(jnp.float32)).astype(x.dtype)


def split_cols(t, sizes):
    idx = [int(v) for v in np.cumsum(sizes)[:-1]]
    return jnp.split(t, idx, axis=-1)


def rope_tables(positions):
    half = QK_ROPE_DIM // 2
    inv = ROPE_THETA ** (-jnp.arange(half, dtype=jnp.float32) / half)
    ang = positions.astype(jnp.float32)[..., None] * inv
    return jnp.cos(ang), jnp.sin(ang)


def apply_rope(t, cos, sin):
    half = QK_ROPE_DIM // 2
    t1, t2 = t[..., :half], t[..., half:]
    return jnp.concatenate([t1 * cos - t2 * sin, t1 * sin + t2 * cos], axis=-1).astype(t.dtype)


def mla_attention(q_lat, kv_lat, k_rope, positions, g_q, g_kv, w_uq, w_ukv):
    B, S, _ = q_lat.shape
    q = (rms_norm(q_lat, g_q) @ w_uq).reshape(B, S, MLA_HEADS, QK_NOPE_DIM + QK_ROPE_DIM)
    q_nope, q_rope = q[..., :QK_NOPE_DIM], q[..., QK_NOPE_DIM:]
    kv = (rms_norm(kv_lat, g_kv) @ w_ukv).reshape(B, S, MLA_HEADS, QK_NOPE_DIM + V_HEAD_DIM)
    k_nope, v = kv[..., :QK_NOPE_DIM], kv[..., QK_NOPE_DIM:]
    cos, sin = rope_tables(positions)
    q_rope = apply_rope(q_rope, cos[:, :, None, :], sin[:, :, None, :])
    k_rope = apply_rope(k_rope, cos, sin)
    scale = (QK_NOPE_DIM + QK_ROPE_DIM) ** -0.5
    outs = []
    for i in range(S // Q_BLOCK):
        q0, q1 = i * Q_BLOCK, (i + 1) * Q_BLOCK
        s = (jnp.einsum('bqhd,bkhd->bhqk', q_nope[:, q0:q1], k_nope[:, :q1]).astype(jnp.float32)
             + jnp.einsum('bqhr,bkr->bhqk', q_rope[:, q0:q1], k_rope[:, :q1]).astype(jnp.float32)) * scale
        causal = jnp.arange(q1)[None, :] <= jnp.arange(q0, q1)[:, None]
        p = jax.nn.softmax(jnp.where(causal, s, -jnp.inf), axis=-1).astype(v.dtype)
        outs.append(jnp.einsum('bhqk,bkhd->bqhd', p, v[:, :q1]))
    return jnp.concatenate(outs, axis=1).reshape(B, S, MLA_HEADS * V_HEAD_DIM)


def causal_depthwise_conv(u, w, b):
    S = u.shape[1]
    up = jnp.pad(u, ((0, 0), (CONV_WIDTH - 1, 0), (0, 0)))
    out = b
    for k in range(CONV_WIDTH):
        out = out + up[:, k:k + S] * w[k]
    return out


def ssd_chunked(xh, dt, A, Bm, Cm):
    Bsz, S, G, Hg, P = xh.shape
    N = Bm.shape[-1]
    L = SSD_CHUNK
    nc = S // L
    xdt = (xh * dt[..., None]).reshape(Bsz, nc, L, G, Hg, P)
    a_cum = jnp.cumsum((dt * A).reshape(Bsz, nc, L, G, Hg), axis=2)
    Bc = Bm.reshape(Bsz, nc, L, G, N).astype(jnp.float32)
    Cc = Cm.reshape(Bsz, nc, L, G, N).astype(jnp.float32)
    seg = a_cum[:, :, :, None] - a_cum[:, :, None, :]
    tri = jnp.tril(jnp.ones((L, L), dtype=bool))[:, :, None, None]
    decay = jnp.exp(jnp.where(tri, seg, -jnp.inf))
    cb = jnp.einsum('bclgn,bcsgn->bclsg', Cc, Bc)
    y_diag = jnp.einsum('bclsg,bclsgh,bcsghp->bclghp', cb, decay, xdt)
    decay_to_end = jnp.exp(a_cum[:, :, -1:] - a_cum)
    states = jnp.einsum('bclgn,bclgh,bclghp->bcghpn', Bc, decay_to_end, xdt)
    chunk_decay = jnp.exp(a_cum[:, :, -1])

    def step(h, inp):
        s_c, d_c = inp
        return h * d_c[..., None, None] + s_c, h

    h0 = jnp.zeros((Bsz, G, Hg, P, N), dtype=states.dtype)
    _, h_prev = lax.scan(step, h0, (jnp.moveaxis(states, 1, 0), jnp.moveaxis(chunk_decay, 1, 0)))
    h_prev = jnp.moveaxis(h_prev, 0, 1)
    y_off = jnp.einsum('bclgn,bcghpn,bclgh->bclghp', Cc, h_prev, jnp.exp(a_cum))
    return (y_diag + y_off).reshape(Bsz, S, G, Hg, P)


def mamba2_mixer(z, xbc, dt_raw, conv_w, conv_b, dt_bias, a_log, d_skip, g_ssm_out):
    B, S, _ = z.shape
    xbc = jax.nn.silu(causal_depthwise_conv(xbc, conv_w, conv_b))
    xs, Bm, Cm = split_cols(xbc, (SSM_D_INNER, SSM_GROUPS * SSM_STATE, SSM_GROUPS * SSM_STATE))
    xh = xs.reshape(B, S, SSM_GROUPS, SSM_HEADS_PER_GROUP, SSM_HEAD_DIM)
    Bm = Bm.reshape(B, S, SSM_GROUPS, SSM_STATE)
    Cm = Cm.reshape(B, S, SSM_GROUPS, SSM_STATE)
    dt = jax.nn.softplus(dt_raw.astype(jnp.float32).reshape(B, S, SSM_GROUPS, SSM_HEADS_PER_GROUP)
                         + dt_bias.astype(jnp.float32).reshape(SSM_GROUPS, SSM_HEADS_PER_GROUP))
    A = -jnp.exp(a_log.astype(jnp.float32)).reshape(SSM_GROUPS, SSM_HEADS_PER_GROUP)
    y = ssd_chunked(xh, dt, A, Bm, Cm)
    y = y + d_skip.reshape(SSM_GROUPS, SSM_HEADS_PER_GROUP)[:, :, None] * xh
    y = y.reshape(B, S, SSM_D_INNER).astype(z.dtype)
    yg = (y * jax.nn.silu(z)).reshape(B, S, SSM_GROUPS, SSM_D_INNER // SSM_GROUPS)
    yg = rms_norm(yg, g_ssm_out.reshape(SSM_GROUPS, SSM_D_INNER // SSM_GROUPS))
    return yg.reshape(B, S, SSM_D_INNER)


def setup_inputs(seed: int = 0) -> dict:
    key = jax.random.key(seed)
    ks = jax.random.split(key, 32)
    f32 = jnp.float32

    def nrm(k, shape, scale):
        return jax.random.normal(k, shape, f32) * scale

    def gain(k, shape):
        return 1.0 + 0.05 * jax.random.normal(k, shape, f32)

    Ld = DEPTH
    x = jax.random.normal(ks[0], (BATCH, SEQ, D_MODEL), f32)
    c = jax.random.normal(ks[1], (BATCH, D_MODEL), f32)
    offs = jax.random.randint(ks[2], (BATCH, 1), 0, 2048, dtype=jnp.int32)
    positions = (offs + jnp.arange(SEQ, dtype=jnp.int32)[None, :]).astype(jnp.int32)
    dt0 = jnp.exp(jax.random.uniform(ks[16], (Ld, SSM_HEADS), f32, np.log(1e-3), np.log(1e-1)))
    return {
        "x": x,
        "c": c,
        "positions": positions,
        "w_ada": nrm(ks[3], (Ld, D_MODEL, 6 * D_MODEL), 0.5 * D_MODEL ** -0.5),
        "b_ada": nrm(ks[4], (Ld, 6 * D_MODEL), 0.01),
        "g_pre_mix": gain(ks[5], (Ld, D_MODEL)),
        "g_post_mix": gain(ks[6], (Ld, D_MODEL)),
        "w_in": nrm(ks[7], (Ld, D_MODEL, IN_COLS), D_MODEL ** -0.5),
        "g_q_lat": gain(ks[8], (Ld, Q_LORA_RANK)),
        "g_kv_lat": gain(ks[9], (Ld, KV_LORA_RANK)),
        "w_uq": nrm(ks[10], (Ld, Q_LORA_RANK, MLA_HEADS * (QK_NOPE_DIM + QK_ROPE_DIM)), Q_LORA_RANK ** -0.5),
        "w_ukv": nrm(ks[11], (Ld, KV_LORA_RANK, MLA_HEADS * (QK_NOPE_DIM + V_HEAD_DIM)), KV_LORA_RANK ** -0.5),
        "w_o_attn": nrm(ks[12], (Ld, MLA_HEADS * V_HEAD_DIM, D_MODEL), (MLA_HEADS * V_HEAD_DIM) ** -0.5),
        "conv_w": nrm(ks[13], (Ld, CONV_WIDTH, CONV_CH), CONV_WIDTH ** -0.5),
        "conv_b": nrm(ks[14], (Ld, CONV_CH), 0.01),
        "dt_bias": dt0 + jnp.log(-jnp.expm1(-dt0)),
        "a_log": jnp.log(jax.random.uniform(ks[17], (Ld, SSM_HEADS), f32, 1.0, 16.0)),
        "d_skip": gain(ks[18], (Ld, SSM_HEADS)),
        "g_ssm_out": gain(ks[19], (Ld, SSM_D_INNER)),
        "w_o_ssm": nrm(ks[20], (Ld, SSM_D_INNER, D_MODEL), SSM_D_INNER ** -0.5),
        "w_out": nrm(ks[21], (Ld, D_MODEL, D_MODEL), D_MODEL ** -0.5),
        "g_pre_mlp": gain(ks[22], (Ld, D_MODEL)),
        "g_post_mlp": gain(ks[23], (Ld, D_MODEL)),
        "w_ff1": nrm(ks[24], (Ld, D_MODEL, D_FF), D_MODEL ** -0.5),
        "w_ff2": nrm(ks[25], (Ld, D_FF, D_MODEL), D_FF ** -0.5),
    }


def reference(x, c, positions, w_ada, b_ada, g_pre_mix, g_post_mix, w_in, g_q_lat, g_kv_lat,
              w_uq, w_ukv, w_o_attn, conv_w, conv_b, dt_bias, a_log, d_skip, g_ssm_out,
              w_o_ssm, w_out, g_pre_mlp, g_post_mlp, w_ff1, w_ff2):
    sc = jax.nn.silu(c)
    for l in range(DEPTH):
        mod = sc @ w_ada[l] + b_ada[l]
        shift1, scale1, gate1, shift2, scale2, gate2 = [m[:, None, :] for m in jnp.split(mod, 6, axis=-1)]
        h = rms_norm(x, g_pre_mix[l]) * (1.0 + scale1) + shift1
        q_lat, kv_lat, k_rope, z, xbc, dt_raw, gate_a, gate_b = split_cols(h @ w_in[l], IN_SIZES)
        attn = mla_attention(q_lat, kv_lat, k_rope, positions, g_q_lat[l], g_kv_lat[l], w_uq[l], w_ukv[l]) @ w_o_attn[l]
        ssm = mamba2_mixer(z, xbc, dt_raw, conv_w[l], conv_b[l], dt_bias[l], a_log[l], d_skip[l], g_ssm_out[l]) @ w_o_ssm[l]
        merged = jax.nn.sigmoid(gate_a) * attn + jax.nn.sigmoid(gate_b) * ssm
        x = x + gate1 * rms_norm(merged @ w_out[l], g_post_mix[l])
        h2 = rms_norm(x, g_pre_mlp[l]) * (1.0 + scale2) + shift2
        ff = jnp.square(jax.nn.relu(h2 @ w_ff1[l])) @ w_ff2[l]
        x = x + gate2 * rms_norm(ff, g_post_mlp[l])
    return x
```

```python
import functools
import math

import jax
import jax.numpy as jnp
from jax import lax
from jax.experimental import pallas as pl
from jax.experimental.pallas import tpu as pltpu

F32 = jnp.float32
BF16 = jnp.bfloat16
HIGHEST = lax.Precision.HIGHEST

N_HEADS = 8
D_NOPE = 128
D_ROPE = 64
D_V = 128
D_QK = D_NOPE + D_ROPE
R_Q = 256
R_KV = 256
ROPE_THETA = 10000.0
SSM_INNER = 2048
SSM_P = 64
SSM_HEADS = 32
SSM_GROUPS = 8
SSM_N = 128
CONV_K = 4
CHUNK = 128
EPS = 1e-6
LOG2E = 1.4426950408889634

VMEM_LIMIT_BYTES = 56 * 1024 * 1024
LANES = 128

_NT = (((1,), (1,)), ((), ()))


def _dot(a, b):
    return jnp.dot(a, b, preferred_element_type=F32)


def _dot_nt(a, b, precision=None):
    return lax.dot_general(a, b, _NT, precision=precision, preferred_element_type=F32)


def _rms(x, g):
    ms = jnp.mean(x * x, axis=-1, keepdims=True)
    return x * lax.rsqrt(ms + EPS) * g


def _sigmoid(x):
    return 1.0 / (1.0 + jnp.exp(-x))


def _const_spec(shape):
    n = len(shape)
    return pl.BlockSpec(shape, lambda *_: (0,) * n, pipeline_mode=pl.Buffered(1))


def _ada_kernel(c_ref, w_ref, b_ref, o_ref):
    c = c_ref[...]
    sc = c * _sigmoid(c)
    o_ref[...] = jnp.dot(sc, w_ref[...], precision=HIGHEST, preferred_element_type=F32) + b_ref[...]


def _ada(c, w_ada, b_ada):
    bsz, d = c.shape
    n = w_ada.shape[1]
    blk = 1024
    return pl.pallas_call(
        _ada_kernel,
        grid=(n // blk,),
        in_specs=[pl.BlockSpec((bsz, d), lambda j: (0, 0)),
                  pl.BlockSpec((d, blk), lambda j: (0, j)),
                  pl.BlockSpec((1, blk), lambda j: (0, j))],
        out_specs=pl.BlockSpec((bsz, blk), lambda j: (0, j)),
        out_shape=jax.ShapeDtypeStruct((bsz, n), F32),
        name="ada",
    )(c, w_ada, b_ada.reshape(1, n))


def _front_kernel(x_ref, pos_ref, mod_ref, gpre_ref, wlat_ref, gq_ref, gkv_ref, wq_ref, wkv_ref,
                  wz_ref, wxbc_ref, convw_ref, convb_ref, dtb_ref, alog_ref, inv_ref,
                  q_ref, k_ref, v_ref, z_ref, xs_ref, bm_ref, cm_ref, dta_ref,
                  carry_ref, cbuf_ref, *, tm, cw, qscale):
    i = pl.program_id(1)

    @pl.when(i == 0)
    def _():
        carry_ref[...] = jnp.zeros_like(carry_ref)

    x = x_ref[...]
    shift1 = mod_ref[0:1, :]
    scale1 = mod_ref[1:2, :]
    h = _rms(x, gpre_ref[...]) * (1.0 + scale1) + shift1
    hb = h.astype(BF16)

    lat = _dot(hb, wlat_ref[...])
    qn = _rms(lat[:, 0:R_Q], gq_ref[...]).astype(BF16)
    kvn = _rms(lat[:, R_Q:R_Q + R_KV], gkv_ref[...]).astype(BF16)
    kr = lat[:, 512:640]
    dtr = lat[:, 640:768]

    lane = lax.broadcasted_iota(jnp.int32, (1, LANES), 1)
    pre = dtr + dtb_ref[...]
    sp = jnp.maximum(pre, 0.0) + jnp.log1p(jnp.exp(-jnp.abs(pre)))
    arow = jnp.where(lane < 32, 1.0, jnp.where(lane < 64, -jnp.exp(alog_ref[...]), 0.0))
    dta_ref[...] = sp * arow

    ang = pos_ref[...].astype(F32) * inv_ref[...]
    cs = jnp.cos(ang)
    sn = jnp.sin(ang)
    rot = jnp.where(lane < 64, cs, jnp.where(lane < 96, -sn, sn))

    def rope(t):
        u = t * rot
        return u + pltpu.roll(u, 64, axis=1)

    kro = rope(kr)[:, 0:D_ROPE].astype(BF16)

    for hd in range(N_HEADS):
        qh = _dot(qn, wq_ref[hd])
        q_ref[hd, :, 0:D_NOPE] = (qh[:, 0:D_NOPE] * qscale).astype(BF16)
        q_ref[hd, :, D_NOPE:D_QK] = (rope(qh[:, 128:256])[:, 0:D_ROPE] * qscale).astype(BF16)
        kvh = _dot(kvn, wkv_ref[hd])
        k_ref[hd, :, 0:D_NOPE] = kvh[:, 0:D_NOPE].astype(BF16)
        k_ref[hd, :, D_NOPE:D_QK] = kro
        v_ref[hd] = kvh[:, 128:256].astype(BF16)

    for c0 in range(0, SSM_INNER, cw):
        z_ref[:, c0:c0 + cw] = _dot(hb, wz_ref[:, c0:c0 + cw]).astype(BF16)

    n_xbc = wxbc_ref.shape[1]
    for ci, c0 in enumerate(range(0, n_xbc, cw)):
        slot = ci % 2
        u = _dot(hb, wxbc_ref[:, c0:c0 + cw])
        cbuf_ref[slot, 0:8, :] = carry_ref[:, c0:c0 + cw]
        cbuf_ref[slot, 8:8 + tm, :] = u
        acc = convb_ref[:, c0:c0 + cw]
        for kk in range(CONV_K):
            r0 = 8 - (CONV_K - 1) + kk
            acc = acc + cbuf_ref[slot, r0:r0 + tm, :] * convw_ref[kk:kk + 1, c0:c0 + cw]
        carry_ref[:, c0:c0 + cw] = cbuf_ref[slot, tm:tm + 8, :]
        sv = (acc * _sigmoid(acc)).astype(BF16)
        if c0 < SSM_INNER:
            xs_ref[:, c0:c0 + cw] = sv
        elif c0 < SSM_INNER + SSM_GROUPS * SSM_N:
            o0 = c0 - SSM_INNER
            bm_ref[:, o0:o0 + cw] = sv
        else:
            o0 = c0 - SSM_INNER - SSM_GROUPS * SSM_N
            cm_ref[:, o0:o0 + cw] = sv


def _front(x, pos3, mod8, gpre, wlat, gq, gkv, wq, wkv, wz, wxbc, convw, convb, dtb, alog, inv128, *, tm):
    bsz, seq, d = x.shape
    cw = 512
    qscale = float(D_QK ** -0.5 * LOG2E)
    n_bc = SSM_GROUPS * SSM_N
    tok = lambda width: pl.BlockSpec((None, tm, width), lambda b, i: (b, i, 0))
    head = lambda width: pl.BlockSpec((None, N_HEADS, tm, width), lambda b, i: (b, 0, i, 0))
    in_specs = [
        tok(d),
        tok(1),
        pl.BlockSpec((None, 8, d), lambda b, i: (b, 0, 0)),
        _const_spec(gpre.shape), _const_spec(wlat.shape), _const_spec(gq.shape), _const_spec(gkv.shape),
        _const_spec(wq.shape), _const_spec(wkv.shape), _const_spec(wz.shape), _const_spec(wxbc.shape),
        _const_spec(convw.shape), _const_spec(convb.shape), _const_spec(dtb.shape), _const_spec(alog.shape),
        _const_spec(inv128.shape),
    ]
    out_specs = [head(D_QK), head(D_QK), head(D_V), tok(SSM_INNER), tok(SSM_INNER), tok(n_bc), tok(n_bc),
                 tok(LANES)]
    out_shape = [
        jax.ShapeDtypeStruct((bsz, N_HEADS, seq, D_QK), BF16),
        jax.ShapeDtypeStruct((bsz, N_HEADS, seq, D_QK), BF16),
        jax.ShapeDtypeStruct((bsz, N_HEADS, seq, D_V), BF16),
        jax.ShapeDtypeStruct((bsz, seq, SSM_INNER), BF16),
        jax.ShapeDtypeStruct((bsz, seq, SSM_INNER), BF16),
        jax.ShapeDtypeStruct((bsz, seq, n_bc), BF16),
        jax.ShapeDtypeStruct((bsz, seq, n_bc), BF16),
        jax.ShapeDtypeStruct((bsz, seq, LANES), F32),
    ]
    return pl.pallas_call(
        functools.partial(_front_kernel, tm=tm, cw=cw, qscale=qscale),
        grid=(bsz, seq // tm),
        in_specs=in_specs,
        out_specs=out_specs,
        out_shape=out_shape,
        scratch_shapes=[pltpu.VMEM((8, wxbc.shape[1]), F32), pltpu.VMEM((2, tm + 8, cw), F32)],
        compiler_params=pltpu.CompilerParams(dimension_semantics=("parallel", "arbitrary"),
                                             vmem_limit_bytes=VMEM_LIMIT_BYTES),
        name="front",
    )(x, pos3, mod8, gpre, wlat, gq, gkv, wq, wkv, wz, wxbc, convw, convb, dtb, alog, inv128)


def _attn_kernel(q_ref, k_ref, v_ref, o_ref, *, tq):
    i = pl.program_id(2)
    q = q_ref[...]

    def update(carry, kb, vb, mask):
        m, l, acc = carry
        s = _dot_nt(q, kb)
        if mask is not None:
            s = jnp.where(mask, s, -jnp.inf)
        m_new = jnp.maximum(m, jnp.max(s, axis=-1, keepdims=True))
        alpha = jnp.exp2(m - m_new)
        p = jnp.exp2(s - m_new)
        l = alpha * l + jnp.sum(p, axis=-1, keepdims=True)
        acc = alpha * acc + _dot(p.astype(BF16), vb)
        return m_new, l, acc

    def body(j, carry):
        off = pl.multiple_of(j * tq, tq)
        return update(carry, k_ref[pl.ds(off, tq), :], v_ref[pl.ds(off, tq), :], None)

    init = (jnp.full((tq, 1), -jnp.inf, F32), jnp.zeros((tq, 1), F32), jnp.zeros((tq, D_V), F32))
    carry = lax.fori_loop(0, i, body, init)
    off = pl.multiple_of(i * tq, tq)
    row = lax.broadcasted_iota(jnp.int32, (tq, tq), 0)
    col = lax.broadcasted_iota(jnp.int32, (tq, tq), 1)
    m, l, acc = update(carry, k_ref[pl.ds(off, tq), :], v_ref[pl.ds(off, tq), :], col <= row)
    o_ref[...] = (acc / l).astype(BF16)


def _attention(q, k, v, *, tq):
    bsz, nh, seq, _ = q.shape
    return pl.pallas_call(
        functools.partial(_attn_kernel, tq=tq),
        grid=(bsz, nh, seq // tq),
        in_specs=[pl.BlockSpec((None, None, tq, D_QK), lambda b, h, i: (b, h, i, 0)),
                  pl.BlockSpec((None, None, seq, D_QK), lambda b, h, i: (b, h, 0, 0)),
                  pl.BlockSpec((None, None, seq, D_V), lambda b, h, i: (b, h, 0, 0))],
        out_specs=pl.BlockSpec((None, tq, D_V), lambda b, h, i: (b, i, h)),
        out_shape=jax.ShapeDtypeStruct((bsz, seq, nh * D_V), BF16),
        compiler_params=pltpu.CompilerParams(dimension_semantics=("parallel", "parallel", "arbitrary"),
                                             vmem_limit_bytes=VMEM_LIMIT_BYTES),
        name="attn",
    )(q, k, v)


def _ssd_kernel(xs_ref, bm_ref, cm_ref, dta_ref, z_ref, dsk_ref, gss_ref, o_ref,
                state_ref, rows_ref):
    ci = pl.program_id(1)

    @pl.when(ci == 0)
    def _():
        state_ref[...] = jnp.zeros_like(state_ref)

    L = CHUNK
    r_i = lax.broadcasted_iota(jnp.int32, (L, L), 0)
    c_i = lax.broadcasted_iota(jnp.int32, (L, L), 1)
    tril = c_i <= r_i
    eye = (r_i == c_i).astype(F32)
    eye_b = eye.astype(BF16)
    lane = lax.broadcasted_iota(jnp.int32, (1, L), 1)

    dta = dta_ref[...]
    cum = jnp.dot(tril.astype(F32), dta, precision=HIGHEST, preferred_element_type=F32)
    acm = jnp.where((lane >= 32) & (lane < 64), cum, 0.0)
    dta_t = _dot_nt(eye, dta, HIGHEST)
    acm_t = _dot_nt(eye, acm, HIGHEST)
    a_last = acm[L - 1:L, :]
    a_last_t = acm_t[:, L - 1:L]
    rows_ref[0] = acm_t
    rows_ref[1] = jnp.exp(a_last_t - acm_t)
    rows_ref[2] = dta_t
    cd_all = jnp.exp(a_last)

    for g in range(SSM_GROUPS):
        cg = cm_ref[:, g * SSM_N:(g + 1) * SSM_N]
        bg = bm_ref[:, g * SSM_N:(g + 1) * SSM_N]
        cb = _dot_nt(cg, bg)
        bt = _dot_nt(eye_b, bg)
        cg32 = cg.astype(F32)
        ys = []
        for pr in range(2):
            hp = g * 2 + pr
            xpair = xs_ref[:, hp * 128:(hp + 1) * 128]
            s_old = state_ref[hp]
            rhs = jnp.concatenate([xpair, s_old.astype(BF16)], axis=0)
            lhs_parts = []
            bw_parts = []
            for e in range(2):
                hh = hp * 2 + e
                colb = jnp.broadcast_to(acm[:, 32 + hh:33 + hh], (L, L))
                a_row = rows_ref[0, 32 + hh:33 + hh, :]
                dte_row = rows_ref[1, 32 + hh:33 + hh, :]
                dt_row = rows_ref[2, hh:hh + 1, :]
                dec = jnp.exp(jnp.where(tril, colb - a_row, -jnp.inf))
                mh = (cb * dec * dt_row).astype(BF16)
                ce = (cg32 * jnp.exp(colb)).astype(BF16)
                lhs_parts.append(jnp.concatenate([mh, ce], axis=1))
                bw_parts.append((bt * (dte_row * dt_row)).astype(BF16))
            yy = _dot(jnp.concatenate(lhs_parts, axis=0), rhs)
            ss = _dot(jnp.concatenate(bw_parts, axis=0), xpair)
            first = lane < 64
            ys.append(jnp.where(first, yy[0:L], yy[L:2 * L]))
            h0 = hp * 2
            cd0 = cd_all[:, 32 + h0:33 + h0]
            cd1 = cd_all[:, 33 + h0:34 + h0]
            cdrow = jnp.where(first, cd0, cd1)
            state_ref[hp] = cdrow * s_old + jnp.where(first, ss[0:SSM_N], ss[SSM_N:2 * SSM_N])
        sl = slice(g * 256, (g + 1) * 256)
        xg = xs_ref[:, sl].astype(F32)
        y = jnp.concatenate(ys, axis=1) + dsk_ref[:, sl] * xg
        zg = z_ref[:, sl].astype(F32)
        yz = y * (zg * _sigmoid(zg))
        o_ref[:, sl] = _rms(yz, gss_ref[:, sl]).astype(BF16)


def _ssd(xs, bm, cm, dta, z, dsk, gss):
    bsz, seq, _ = xs.shape
    n_bc = SSM_GROUPS * SSM_N
    tok = lambda width: pl.BlockSpec((None, CHUNK, width), lambda b, c: (b, c, 0))
    return pl.pallas_call(
        _ssd_kernel,
        grid=(bsz, seq // CHUNK),
        in_specs=[tok(SSM_INNER), tok(n_bc), tok(n_bc), tok(LANES), tok(SSM_INNER),
                  _const_spec(dsk.shape), _const_spec(gss.shape)],
        out_specs=tok(SSM_INNER),
        out_shape=jax.ShapeDtypeStruct((bsz, seq, SSM_INNER), BF16),
        scratch_shapes=[pltpu.VMEM((SSM_HEADS // 2, SSM_N, 128), F32), pltpu.VMEM((3, CHUNK, CHUNK), F32)],
        compiler_params=pltpu.CompilerParams(dimension_semantics=("parallel", "arbitrary"),
                                             vmem_limit_bytes=VMEM_LIMIT_BYTES),
        name="ssd",
    )(xs, bm, cm, dta, z, dsk, gss)


def _merge_kernel(x_ref, attn_ref, yg_ref, mod_ref, gpre_ref, wga_ref, wgb_ref, woa_ref, wob_ref,
                  wout_ref, gpost_ref, o_ref, mb_ref, *, cw):
    x = x_ref[...]
    shift1 = mod_ref[0:1, :]
    scale1 = mod_ref[1:2, :]
    gate1 = mod_ref[2:3, :]
    hb = (_rms(x, gpre_ref[...]) * (1.0 + scale1) + shift1).astype(BF16)
    at = attn_ref[...]
    yg = yg_ref[...]
    d = x.shape[1]
    for c0 in range(0, d, cw):
        sl = slice(c0, c0 + cw)
        ga = _sigmoid(_dot(hb, wga_ref[:, sl]))
        gb = _sigmoid(_dot(hb, wgb_ref[:, sl]))
        merged = ga * _dot(at, woa_ref[:, sl]) + gb * _dot(yg, wob_ref[:, sl])
        mb_ref[:, sl] = merged.astype(BF16)
    mix = _dot(mb_ref[...], wout_ref[...])
    o_ref[...] = x + gate1 * _rms(mix, gpost_ref[...])


def _merge(x, attn, yg, mod8, gpre, wga, wgb, woa, wob, wout, gpost, *, tm):
    bsz, seq, d = x.shape
    tok = lambda width: pl.BlockSpec((None, tm, width), lambda b, i: (b, i, 0))
    return pl.pallas_call(
        functools.partial(_merge_kernel, cw=512),
        grid=(bsz, seq // tm),
        in_specs=[tok(d), tok(attn.shape[2]), tok(yg.shape[2]),
                  pl.BlockSpec((None, 8, d), lambda b, i: (b, 0, 0)),
                  _const_spec(gpre.shape), _const_spec(wga.shape), _const_spec(wgb.shape),
                  _const_spec(woa.shape), _const_spec(wob.shape), _const_spec(wout.shape),
                  _const_spec(gpost.shape)],
        out_specs=tok(d),
        out_shape=jax.ShapeDtypeStruct((bsz, seq, d), F32),
        scratch_shapes=[pltpu.VMEM((tm, d), BF16)],
        compiler_params=pltpu.CompilerParams(dimension_semantics=("parallel", "parallel"),
                                             vmem_limit_bytes=VMEM_LIMIT_BYTES),
        name="merge",
    )(x, attn, yg, mod8, gpre, wga, wgb, woa, wob, wout, gpost)


def _mlp_kernel(x_ref, mod_ref, gpre_ref, w1_ref, w2_ref, gpost_ref, o_ref, acc_ref, *, cw):
    x = x_ref[...]
    shift2 = mod_ref[3:4, :]
    scale2 = mod_ref[4:5, :]
    gate2 = mod_ref[5:6, :]
    hb = (_rms(x, gpre_ref[...]) * (1.0 + scale2) + shift2).astype(BF16)
    dff = w1_ref.shape[1]
    for ci, c0 in enumerate(range(0, dff, cw)):
        a = jnp.maximum(_dot(hb, w1_ref[:, c0:c0 + cw]), 0.0)
        part = _dot((a * a).astype(BF16), w2_ref[c0:c0 + cw, :])
        if ci == 0:
            acc_ref[...] = part
        else:
            acc_ref[...] += part
    o_ref[...] = x + gate2 * _rms(acc_ref[...], gpost_ref[...])


def _mlp(x, mod8, gpre, w1, w2, gpost, *, tm):
    bsz, seq, d = x.shape
    tok = lambda width: pl.BlockSpec((None, tm, width), lambda b, i: (b, i, 0))
    return pl.pallas_call(
        functools.partial(_mlp_kernel, cw=1024),
        grid=(bsz, seq // tm),
        in_specs=[tok(d), pl.BlockSpec((None, 8, d), lambda b, i: (b, 0, 0)),
                  _const_spec(gpre.shape), _const_spec(w1.shape), _const_spec(w2.shape),
                  _const_spec(gpost.shape)],
        out_specs=tok(d),
        out_shape=jax.ShapeDtypeStruct((bsz, seq, d), F32),
        scratch_shapes=[pltpu.VMEM((tm, d), F32)],
        compiler_params=pltpu.CompilerParams(dimension_semantics=("parallel", "parallel"),
                                             vmem_limit_bytes=VMEM_LIMIT_BYTES),
        name="mlp",
    )(x, mod8, gpre, w1, w2, gpost)


def _pad_row(v, width, offset=0):
    out = jnp.zeros((1, width), F32)
    return out.at[0, offset:offset + v.shape[0]].set(v.astype(F32))


def kernel(x, c, positions, w_ada, b_ada, g_pre_mix, g_post_mix, w_in, g_q_lat, g_kv_lat, w_uq, w_ukv,
           w_o_attn, conv_w, conv_b, dt_bias, a_log, d_skip, g_ssm_out, w_o_ssm, w_out, g_pre_mlp,
           g_post_mlp, w_ff1, w_ff2):
    bsz, seq, d = x.shape
    depth = w_ada.shape[0]
    tm = min(512, seq)
    tq = min(512, seq)
    half = D_ROPE // 2
    inv = ROPE_THETA ** (-jnp.arange(half, dtype=F32) / half)
    inv128 = jnp.tile(inv, 4).reshape(1, LANES)
    pos3 = positions.reshape(bsz, seq, 1)
    row = lambda v: v.reshape(1, -1).astype(F32)

    for l in range(depth):
        wi = w_in[l]
        o_q, o_kv, o_kr, o_z = 0, R_Q, R_Q + R_KV, R_Q + R_KV + D_ROPE
        o_xbc = o_z + SSM_INNER
        o_dt = o_xbc + SSM_INNER + 2 * SSM_GROUPS * SSM_N
        o_ga = o_dt + SSM_HEADS
        o_gb = o_ga + d
        kr1 = wi[:, o_kr:o_kr + half]
        kr2 = wi[:, o_kr + half:o_kr + D_ROPE]
        wdt = wi[:, o_dt:o_dt + SSM_HEADS]
        wlat = jnp.concatenate(
            [wi[:, o_q:o_kr], kr1, kr2, kr2, kr1, wdt, wdt, jnp.zeros((d, 64), F32)], axis=1).astype(BF16)
        wz = wi[:, o_z:o_xbc].astype(BF16)
        wxbc = wi[:, o_xbc:o_dt].astype(BF16)
        wga = wi[:, o_ga:o_gb].astype(BF16)
        wgb = wi[:, o_gb:o_gb + d].astype(BF16)
        wq3 = w_uq[l].reshape(R_Q, N_HEADS, D_QK)
        q1 = wq3[:, :, D_NOPE:D_NOPE + half]
        q2 = wq3[:, :, D_NOPE + half:]
        wq = jnp.concatenate([wq3[:, :, :D_NOPE], q1, q2, q2, q1], axis=2).transpose(1, 0, 2).astype(BF16)
        wkv = w_ukv[l].reshape(R_KV, N_HEADS, D_NOPE + D_V).transpose(1, 0, 2).astype(BF16)
        dtb = jnp.concatenate([row(dt_bias[l]), row(dt_bias[l]), jnp.zeros((1, 64), F32)], axis=1)
        alog = _pad_row(a_log[l], LANES, 32)
        dsk = jnp.repeat(d_skip[l].astype(F32), SSM_P).reshape(1, SSM_INNER)

        mod = _ada(c, w_ada[l], b_ada[l])
        mod8 = jnp.concatenate([mod.reshape(bsz, 6, d), jnp.zeros((bsz, 2, d), F32)], axis=1)

        q, k, v, z, xs, bm, cm, dta = _front(
            x, pos3, mod8, row(g_pre_mix[l]), wlat, row(g_q_lat[l]), row(g_kv_lat[l]), wq, wkv, wz, wxbc,
            conv_w[l].astype(F32), row(conv_b[l]), dtb, alog, inv128, tm=tm)
        attn = _attention(q, k, v, tq=tq)
        yg = _ssd(xs, bm, cm, dta, z, dsk, row(g_ssm_out[l]))
        x = _merge(x, attn, yg, mod8, row(g_pre_mix[l]), wga, wgb, w_o_attn[l].astype(BF16),
                   w_o_ssm[l].astype(BF16), w_out[l].astype(BF16), row(g_post_mix[l]), tm=tm)
        x = _mlp(x, mod8, row(g_pre_mlp[l]), w_ff1[l].astype(BF16), w_ff2[l].astype(BF16),
                 row(g_post_mlp[l]), tm=tm)
    return x
```

```python
import functools
import math

import jax
import jax.numpy as jnp
from jax import lax
from jax.experimental import pallas as pl
from jax.experimental.pallas import tpu as pltpu

F32 = jnp.float32
BF16 = jnp.bfloat16
HIGHEST = lax.Precision.HIGHEST

N_HEADS = 8
D_NOPE = 128
D_ROPE = 64
D_V = 128
D_QK = D_NOPE + D_ROPE
R_Q = 256
R_KV = 256
ROPE_THETA = 10000.0
SSM_INNER = 2048
SSM_P = 64
SSM_HEADS = 32
SSM_GROUPS = 8
SSM_N = 128
CONV_K = 4
CHUNK = 128
EPS = 1e-6
LOG2E = 1.4426950408889634

VMEM_LIMIT_BYTES = 56 * 1024 * 1024
LANES = 128

_NT = (((1,), (1,)), ((), ()))


def _dot(a, b):
    return jnp.dot(a, b, preferred_element_type=F32)


def _dot_nt(a, b, precision=None):
    return lax.dot_general(a, b, _NT, precision=precision, preferred_element_type=F32)


def _rms(x, g):
    ms = jnp.mean(x * x, axis=-1, keepdims=True)
    return x * lax.rsqrt(ms + EPS) * g


def _sigmoid(x):
    return 1.0 / (1.0 + jnp.exp(-x))


def _const_spec(shape):
    n = len(shape)
    return pl.BlockSpec(shape, lambda *_: (0,) * n, pipeline_mode=pl.Buffered(1))


def _ada_kernel(c_ref, w_ref, b_ref, o_ref):
    c = c_ref[...]
    sc = c * _sigmoid(c)
    o_ref[...] = jnp.dot(sc, w_ref[...], precision=HIGHEST, preferred_element_type=F32) + b_ref[...]


def _ada(c, w_ada, b_ada):
    bsz, d = c.shape
    n = w_ada.shape[1]
    blk = 1024
    return pl.pallas_call(
        _ada_kernel,
        grid=(n // blk,),
        in_specs=[pl.BlockSpec((bsz, d), lambda j: (0, 0)),
                  pl.BlockSpec((d, blk), lambda j: (0, j)),
                  pl.BlockSpec((1, blk), lambda j: (0, j))],
        out_specs=pl.BlockSpec((bsz, blk), lambda j: (0, j)),
        out_shape=jax.ShapeDtypeStruct((bsz, n), F32),
        name="ada",
    )(c, w_ada, b_ada.reshape(1, n))


def _front_kernel(x_ref, pos_ref, mod_ref, gpre_ref, wlat_ref, gq_ref, gkv_ref, wq_ref, wkv_ref,
                  wz_ref, wxbc_ref, convw_ref, convb_ref, dtb_ref, alog_ref, inv_ref,
                  q_ref, k_ref, v_ref, z_ref, xs_ref, bm_ref, cm_ref, dta_ref,
                  carry_ref, cbuf_ref, *, tm, cw, qscale):
    i = pl.program_id(1)

    @pl.when(i == 0)
    def _():
        carry_ref[...] = jnp.zeros_like(carry_ref)

    x = x_ref[...]
    shift1 = mod_ref[0:1, :]
    scale1 = mod_ref[1:2, :]
    h = _rms(x, gpre_ref[...]) * (1.0 + scale1) + shift1
    hb = h.astype(BF16)

    lat = _dot(hb, wlat_ref[...])
    qn = _rms(lat[:, 0:R_Q], gq_ref[...]).astype(BF16)
    kvn = _rms(lat[:, R_Q:R_Q + R_KV], gkv_ref[...]).astype(BF16)
    kr = lat[:, 512:640]
    dtr = lat[:, 640:768]

    lane = lax.broadcasted_iota(jnp.int32, (1, LANES), 1)
    pre = dtr + dtb_ref[...]
    sp = jnp.maximum(pre, 0.0) + jnp.log1p(jnp.exp(-jnp.abs(pre)))
    arow = jnp.where(lane < 32, 1.0, jnp.where(lane < 64, -jnp.exp(alog_ref[...]), 0.0))
    dta_ref[...] = sp * arow

    ang = pos_ref[...].astype(F32) * inv_ref[...]
    cs = jnp.cos(ang)
    sn = jnp.sin(ang)
    rot = jnp.where(lane < 64, cs, jnp.where(lane < 96, -sn, sn))

    def rope(t):
        u = t * rot
        return u + pltpu.roll(u, 64, axis=1)

    kro = rope(kr)[:, 0:D_ROPE].astype(BF16)

    for hd in range(N_HEADS):
        qh = _dot(qn, wq_ref[hd])
        q_ref[hd, :, 0:D_NOPE] = (qh[:, 0:D_NOPE] * qscale).astype(BF16)
        q_ref[hd, :, D_NOPE:D_QK] = (rope(qh[:, 128:256])[:, 0:D_ROPE] * qscale).astype(BF16)
        kvh = _dot(kvn, wkv_ref[hd])
        k_ref[hd, :, 0:D_NOPE] = kvh[:, 0:D_NOPE].astype(BF16)
        k_ref[hd, :, D_NOPE:D_QK] = kro
        v_ref[hd] = kvh[:, 128:256].astype(BF16)

    for c0 in range(0, SSM_INNER, cw):
        z_ref[:, c0:c0 + cw] = _dot(hb, wz_ref[:, c0:c0 + cw]).astype(BF16)

    n_xbc = wxbc_ref.shape[1]
    for ci, c0 in enumerate(range(0, n_xbc, cw)):
        slot = ci % 2
        u = _dot(hb, wxbc_ref[:, c0:c0 + cw])
        cbuf_ref[slot, 0:8, :] = carry_ref[:, c0:c0 + cw]
        cbuf_ref[slot, 8:8 + tm, :] = u
        acc = convb_ref[:, c0:c0 + cw]
        for kk in range(CONV_K):
            r0 = 8 - (CONV_K - 1) + kk
            acc = acc + cbuf_ref[slot, r0:r0 + tm, :] * convw_ref[kk:kk + 1, c0:c0 + cw]
        carry_ref[:, c0:c0 + cw] = cbuf_ref[slot, tm:tm + 8, :]
        sv = (acc * _sigmoid(acc)).astype(BF16)
        if c0 < SSM_INNER:
            xs_ref[:, c0:c0 + cw] = sv
        elif c0 < SSM_INNER + SSM_GROUPS * SSM_N:
            o0 = c0 - SSM_INNER
            bm_ref[:, o0:o0 + cw] = sv
        else:
            o0 = c0 - SSM_INNER - SSM_GROUPS * SSM_N
            cm_ref[:, o0:o0 + cw] = sv


def _front(x, pos3, mod8, gpre, wlat, gq, gkv, wq, wkv, wz, wxbc, convw, convb, dtb, alog, inv128, *, tm):
    bsz, seq, d = x.shape
    cw = 512
    qscale = float(D_QK ** -0.5 * LOG2E)
    n_bc = SSM_GROUPS * SSM_N
    tok = lambda width: pl.BlockSpec((None, tm, width), lambda b, i: (b, i, 0))
    head = lambda width: pl.BlockSpec((None, N_HEADS, tm, width), lambda b, i: (b, 0, i, 0))
    in_specs = [
        tok(d),
        tok(1),
        pl.BlockSpec((None, 8, d), lambda b, i: (b, 0, 0)),
        _const_spec(gpre.shape), _const_spec(wlat.shape), _const_spec(gq.shape), _const_spec(gkv.shape),
        _const_spec(wq.shape), _const_spec(wkv.shape), _const_spec(wz.shape), _const_spec(wxbc.shape),
        _const_spec(convw.shape), _const_spec(convb.shape), _const_spec(dtb.shape), _const_spec(alog.shape),
        _const_spec(inv128.shape),
    ]
    out_specs = [head(D_QK), head(D_QK), head(D_V), tok(SSM_INNER), tok(SSM_INNER), tok(n_bc), tok(n_bc),
                 tok(LANES)]
    out_shape = [
        jax.ShapeDtypeStruct((bsz, N_HEADS, seq, D_QK), BF16),
        jax.ShapeDtypeStruct((bsz, N_HEADS, seq, D_QK), BF16),
        jax.ShapeDtypeStruct((bsz, N_HEADS, seq, D_V), BF16),
        jax.ShapeDtypeStruct((bsz, seq, SSM_INNER), BF16),
        jax.ShapeDtypeStruct((bsz, seq, SSM_INNER), BF16),
        jax.ShapeDtypeStruct((bsz, seq, n_bc), BF16),
        jax.ShapeDtypeStruct((bsz, seq, n_bc), BF16),
        jax.ShapeDtypeStruct((bsz, seq, LANES), F32),
    ]
    return pl.pallas_call(
        functools.partial(_front_kernel, tm=tm, cw=cw, qscale=qscale),
        grid=(bsz, seq // tm),
        in_specs=in_specs,
        out_specs=out_specs,
        out_shape=out_shape,
        scratch_shapes=[pltpu.VMEM((8, wxbc.shape[1]), F32), pltpu.VMEM((2, tm + 8, cw), F32)],
        compiler_params=pltpu.CompilerParams(dimension_semantics=("parallel", "arbitrary"),
                                             vmem_limit_bytes=VMEM_LIMIT_BYTES),
        name="front",
    )(x, pos3, mod8, gpre, wlat, gq, gkv, wq, wkv, wz, wxbc, convw, convb, dtb, alog, inv128)


def _attn_kernel(q_ref, k_ref, v_ref, o_ref, sa_ref, sb_ref, m_ref, l_ref, acc_ref, *, tq, tk):
    i = pl.program_id(2)
    n_rep = tk // LANES

    def scores(rows, blk):
        return _dot_nt(q_ref[rows, :], k_ref[pl.ds(pl.multiple_of(blk * tk, tk), tk), :])

    def update(rows, blk, s):
        m_prev = m_ref[rows, :]
        m_new = jnp.maximum(m_prev, jnp.max(s, axis=-1, keepdims=True))
        alpha = jnp.exp2(m_prev - m_new)
        p = jnp.exp2(s - jnp.tile(m_new, (1, n_rep)))
        m_ref[rows, :] = m_new
        l_ref[rows, :] = alpha * l_ref[rows, :] + jnp.sum(p, axis=-1, keepdims=True)
        return alpha, p.astype(BF16), v_ref[pl.ds(pl.multiple_of(blk * tk, tk), tk), :]

    m_ref[...] = jnp.full(m_ref.shape, -jnp.inf, F32)
    l_ref[...] = jnp.zeros(l_ref.shape, F32)
    acc_ref[...] = jnp.zeros(acc_ref.shape, F32)
    allrows = slice(0, tq)
    bot = slice(tk, tq)
    sa_ref[...] = scores(allrows, 0)

    def step(blk, s_cur_ref, s_next_ref):
        alpha, p, vb = update(allrows, blk, s_cur_ref[...])
        s_next_ref[...] = scores(allrows, blk + 1)
        acc_ref[...] = alpha * acc_ref[...] + _dot(p, vb)

    def body(jj, _):
        step(2 * jj, sa_ref, sb_ref)
        step(2 * jj + 1, sb_ref, sa_ref)
        return 0

    lax.fori_loop(0, i, body, 0)

    row = lax.broadcasted_iota(jnp.int32, (tq, tk), 0)
    col = lax.broadcasted_iota(jnp.int32, (tq, tk), 1)
    alpha, p, vb = update(allrows, 2 * i, jnp.where(col <= row, sa_ref[...], -jnp.inf))
    s_bot = scores(bot, 2 * i + 1)
    acc_ref[...] = alpha * acc_ref[...] + _dot(p, vb)
    row_b = lax.broadcasted_iota(jnp.int32, (tk, tk), 0)
    col_b = lax.broadcasted_iota(jnp.int32, (tk, tk), 1)
    alpha, p, vb = update(bot, 2 * i + 1, jnp.where(col_b <= row_b, s_bot, -jnp.inf))
    acc_ref[bot, :] = alpha * acc_ref[bot, :] + _dot(p, vb)
    o_ref[...] = (acc_ref[...] / l_ref[...]).astype(BF16)


def _attention(q, k, v, *, tq):
    bsz, nh, seq, _ = q.shape
    return pl.pallas_call(
        functools.partial(_attn_kernel, tq=tq, tk=tq // 2),
        grid=(bsz, nh, seq // tq),
        in_specs=[pl.BlockSpec((None, None, tq, D_QK), lambda b, h, i: (b, h, i, 0)),
                  pl.BlockSpec((None, None, seq, D_QK), lambda b, h, i: (b, h, 0, 0)),
                  pl.BlockSpec((None, None, seq, D_V), lambda b, h, i: (b, h, 0, 0))],
        out_specs=pl.BlockSpec((None, tq, D_V), lambda b, h, i: (b, i, h)),
        out_shape=jax.ShapeDtypeStruct((bsz, seq, nh * D_V), BF16),
        scratch_shapes=[pltpu.VMEM((tq, tq // 2), F32), pltpu.VMEM((tq, tq // 2), F32), pltpu.VMEM((tq, LANES), F32),
                        pltpu.VMEM((tq, LANES), F32), pltpu.VMEM((tq, D_V), F32)],
        compiler_params=pltpu.CompilerParams(dimension_semantics=("parallel", "parallel", "arbitrary"),
                                             vmem_limit_bytes=VMEM_LIMIT_BYTES),
        name="attn",
    )(q, k, v)


def _ssd_kernel(xs_ref, bm_ref, cm_ref, dta_ref, z_ref, dsk_ref, gss_ref, o_ref,
                state_ref, rows_ref):
    ci = pl.program_id(1)

    @pl.when(ci == 0)
    def _():
        state_ref[...] = jnp.zeros_like(state_ref)

    L = CHUNK
    r_i = lax.broadcasted_iota(jnp.int32, (L, L), 0)
    c_i = lax.broadcasted_iota(jnp.int32, (L, L), 1)
    tril = c_i <= r_i
    eye = (r_i == c_i).astype(F32)
    eye_b = eye.astype(BF16)
    lane = lax.broadcasted_iota(jnp.int32, (1, L), 1)

    dta = dta_ref[...]
    cum = jnp.dot(tril.astype(F32), dta, precision=HIGHEST, preferred_element_type=F32)
    acm = jnp.where((lane >= 32) & (lane < 64), cum, 0.0)
    dta_t = _dot_nt(eye, dta, HIGHEST)
    acm_t = _dot_nt(eye, acm, HIGHEST)
    a_last = acm[L - 1:L, :]
    a_last_t = acm_t[:, L - 1:L]
    rows_ref[0] = acm_t
    rows_ref[1] = jnp.exp(a_last_t - acm_t)
    rows_ref[2] = dta_t
    cd_all = jnp.exp(a_last)

    for g in range(SSM_GROUPS):
        cg = cm_ref[:, g * SSM_N:(g + 1) * SSM_N]
        bg = bm_ref[:, g * SSM_N:(g + 1) * SSM_N]
        cb = _dot_nt(cg, bg)
        bt = _dot_nt(eye_b, bg)
        cg32 = cg.astype(F32)
        ys = []
        for pr in range(2):
            hp = g * 2 + pr
            xpair = xs_ref[:, hp * 128:(hp + 1) * 128]
            s_old = state_ref[hp]
            rhs = jnp.concatenate([xpair, s_old.astype(BF16)], axis=0)
            lhs_parts = []
            bw_parts = []
            for e in range(2):
                hh = hp * 2 + e
                colb = jnp.broadcast_to(acm[:, 32 + hh:33 + hh], (L, L))
                a_row = rows_ref[0, 32 + hh:33 + hh, :]
                dte_row = rows_ref[1, 32 + hh:33 + hh, :]
                dt_row = rows_ref[2, hh:hh + 1, :]
                dec = jnp.exp(jnp.where(tril, colb - a_row, -jnp.inf))
                mh = (cb * dec * dt_row).astype(BF16)
                ce = (cg32 * jnp.exp(colb)).astype(BF16)
                lhs_parts.append(jnp.concatenate([mh, ce], axis=1))
                bw_parts.append((bt * (dte_row * dt_row)).astype(BF16))
            yy = _dot(jnp.concatenate(lhs_parts, axis=0), rhs)
            ss = _dot(jnp.concatenate(bw_parts, axis=0), xpair)
            first = lane < 64
            ys.append(jnp.where(first, yy[0:L], yy[L:2 * L]))
            h0 = hp * 2
            cd0 = cd_all[:, 32 + h0:33 + h0]
            cd1 = cd_all[:, 33 + h0:34 + h0]
            cdrow = jnp.where(first, cd0, cd1)
            state_ref[hp] = cdrow * s_old + jnp.where(first, ss[0:SSM_N], ss[SSM_N:2 * SSM_N])
        sl = slice(g * 256, (g + 1) * 256)
        xg = xs_ref[:, sl].astype(F32)
        y = jnp.concatenate(ys, axis=1) + dsk_ref[:, sl] * xg
        zg = z_ref[:, sl].astype(F32)
        yz = y * (zg * _sigmoid(zg))
        o_ref[:, sl] = _rms(yz, gss_ref[:, sl]).astype(BF16)


def _ssd(xs, bm, cm, dta, z, dsk, gss):
    bsz, seq, _ = xs.shape
    n_bc = SSM_GROUPS * SSM_N
    tok = lambda width: pl.BlockSpec((None, CHUNK, width), lambda b, c: (b, c, 0))
    return pl.pallas_call(
        _ssd_kernel,
        grid=(bsz, seq // CHUNK),
        in_specs=[tok(SSM_INNER), tok(n_bc), tok(n_bc), tok(LANES), tok(SSM_INNER),
                  _const_spec(dsk.shape), _const_spec(gss.shape)],
        out_specs=tok(SSM_INNER),
        out_shape=jax.ShapeDtypeStruct((bsz, seq, SSM_INNER), BF16),
        scratch_shapes=[pltpu.VMEM((SSM_HEADS // 2, SSM_N, 128), F32), pltpu.VMEM((3, CHUNK, CHUNK), F32)],
        compiler_params=pltpu.CompilerParams(dimension_semantics=("parallel", "arbitrary"),
                                             vmem_limit_bytes=VMEM_LIMIT_BYTES),
        name="ssd",
    )(xs, bm, cm, dta, z, dsk, gss)


def _merge_kernel(x_ref, attn_ref, yg_ref, mod_ref, gpre_ref, wga_ref, wgb_ref, woa_ref, wob_ref,
                  wout_ref, gpost_ref, o_ref, mb_ref, *, cw):
    x = x_ref[...]
    shift1 = mod_ref[0:1, :]
    scale1 = mod_ref[1:2, :]
    gate1 = mod_ref[2:3, :]
    hb = (_rms(x, gpre_ref[...]) * (1.0 + scale1) + shift1).astype(BF16)
    at = attn_ref[...]
    yg = yg_ref[...]
    d = x.shape[1]
    for c0 in range(0, d, cw):
        sl = slice(c0, c0 + cw)
        ga = _sigmoid(_dot(hb, wga_ref[:, sl]))
        gb = _sigmoid(_dot(hb, wgb_ref[:, sl]))
        merged = ga * _dot(at, woa_ref[:, sl]) + gb * _dot(yg, wob_ref[:, sl])
        mb_ref[:, sl] = merged.astype(BF16)
    mix = _dot(mb_ref[...], wout_ref[...])
    o_ref[...] = x + gate1 * _rms(mix, gpost_ref[...])


def _merge(x, attn, yg, mod8, gpre, wga, wgb, woa, wob, wout, gpost, *, tm):
    bsz, seq, d = x.shape
    tok = lambda width: pl.BlockSpec((None, tm, width), lambda b, i: (b, i, 0))
    return pl.pallas_call(
        functools.partial(_merge_kernel, cw=512),
        grid=(bsz, seq // tm),
        in_specs=[tok(d), tok(attn.shape[2]), tok(yg.shape[2]),
                  pl.BlockSpec((None, 8, d), lambda b, i: (b, 0, 0)),
                  _const_spec(gpre.shape), _const_spec(wga.shape), _const_spec(wgb.shape),
                  _const_spec(woa.shape), _const_spec(wob.shape), _const_spec(wout.shape),
                  _const_spec(gpost.shape)],
        out_specs=tok(d),
        out_shape=jax.ShapeDtypeStruct((bsz, seq, d), F32),
        scratch_shapes=[pltpu.VMEM((tm, d), BF16)],
        compiler_params=pltpu.CompilerParams(dimension_semantics=("parallel", "parallel"),
                                             vmem_limit_bytes=VMEM_LIMIT_BYTES),
        name="merge",
    )(x, attn, yg, mod8, gpre, wga, wgb, woa, wob, wout, gpost)


def _mlp_kernel(x_ref, mod_ref, gpre_ref, w1_ref, w2_ref, gpost_ref, o_ref, acc_ref, *, cw):
    x = x_ref[...]
    shift2 = mod_ref[3:4, :]
    scale2 = mod_ref[4:5, :]
    gate2 = mod_ref[5:6, :]
    hb = (_rms(x, gpre_ref[...]) * (1.0 + scale2) + shift2).astype(BF16)
    dff = w1_ref.shape[1]
    for ci, c0 in enumerate(range(0, dff, cw)):
        a = jnp.maximum(_dot(hb, w1_ref[:, c0:c0 + cw]), 0.0)
        part = _dot((a * a).astype(BF16), w2_ref[c0:c0 + cw, :])
        if ci == 0:
            acc_ref[...] = part
        else:
            acc_ref[...] += part
    o_ref[...] = x + gate2 * _rms(acc_ref[...], gpost_ref[...])


def _mlp(x, mod8, gpre, w1, w2, gpost, *, tm):
    bsz, seq, d = x.shape
    tok = lambda width: pl.BlockSpec((None, tm, width), lambda b, i: (b, i, 0))
    return pl.pallas_call(
        functools.partial(_mlp_kernel, cw=1024),
        grid=(bsz, seq // tm),
        in_specs=[tok(d), pl.BlockSpec((None, 8, d), lambda b, i: (b, 0, 0)),
                  _const_spec(gpre.shape), _const_spec(w1.shape), _const_spec(w2.shape),
                  _const_spec(gpost.shape)],
        out_specs=tok(d),
        out_shape=jax.ShapeDtypeStruct((bsz, seq, d), F32),
        scratch_shapes=[pltpu.VMEM((tm, d), F32)],
        compiler_params=pltpu.CompilerParams(dimension_semantics=("parallel", "parallel"),
                                             vmem_limit_bytes=VMEM_LIMIT_BYTES),
        name="mlp",
    )(x, mod8, gpre, w1, w2, gpost)


def _pad_row(v, width, offset=0):
    out = jnp.zeros((1, width), F32)
    return out.at[0, offset:offset + v.shape[0]].set(v.astype(F32))


def kernel(x, c, positions, w_ada, b_ada, g_pre_mix, g_post_mix, w_in, g_q_lat, g_kv_lat, w_uq, w_ukv,
           w_o_attn, conv_w, conv_b, dt_bias, a_log, d_skip, g_ssm_out, w_o_ssm, w_out, g_pre_mlp,
           g_post_mlp, w_ff1, w_ff2):
    bsz, seq, d = x.shape
    depth = w_ada.shape[0]
    tm = min(512, seq)
    tq = min(1024, seq)
    half = D_ROPE // 2
    inv = ROPE_THETA ** (-jnp.arange(half, dtype=F32) / half)
    inv128 = jnp.tile(inv, 4).reshape(1, LANES)
    pos3 = positions.reshape(bsz, seq, 1)
    row = lambda v: v.reshape(1, -1).astype(F32)

    for l in range(depth):
        wi = w_in[l]
        o_q, o_kv, o_kr, o_z = 0, R_Q, R_Q + R_KV, R_Q + R_KV + D_ROPE
        o_xbc = o_z + SSM_INNER
        o_dt = o_xbc + SSM_INNER + 2 * SSM_GROUPS * SSM_N
        o_ga = o_dt + SSM_HEADS
        o_gb = o_ga + d
        kr1 = wi[:, o_kr:o_kr + half]
        kr2 = wi[:, o_kr + half:o_kr + D_ROPE]
        wdt = wi[:, o_dt:o_dt + SSM_HEADS]
        wlat = jnp.concatenate(
            [wi[:, o_q:o_kr], kr1, kr2, kr2, kr1, wdt, wdt, jnp.zeros((d, 64), F32)], axis=1).astype(BF16)
        wz = wi[:, o_z:o_xbc].astype(BF16)
        wxbc = wi[:, o_xbc:o_dt].astype(BF16)
        wga = wi[:, o_ga:o_gb].astype(BF16)
        wgb = wi[:, o_gb:o_gb + d].astype(BF16)
        wq3 = w_uq[l].reshape(R_Q, N_HEADS, D_QK)
        q1 = wq3[:, :, D_NOPE:D_NOPE + half]
        q2 = wq3[:, :, D_NOPE + half:]
        wq = jnp.concatenate([wq3[:, :, :D_NOPE], q1, q2, q2, q1], axis=2).transpose(1, 0, 2).astype(BF16)
        wkv = w_ukv[l].reshape(R_KV, N_HEADS, D_NOPE + D_V).transpose(1, 0, 2).astype(BF16)
        dtb = jnp.concatenate([row(dt_bias[l]), row(dt_bias[l]), jnp.zeros((1, 64), F32)], axis=1)
        alog = _pad_row(a_log[l], LANES, 32)
        dsk = jnp.repeat(d_skip[l].astype(F32), SSM_P).reshape(1, SSM_INNER)

        mod = _ada(c, w_ada[l], b_ada[l])
        mod8 = jnp.concatenate([mod.reshape(bsz, 6, d), jnp.zeros((bsz, 2, d), F32)], axis=1)

        q, k, v, z, xs, bm, cm, dta = _front(
            x, pos3, mod8, row(g_pre_mix[l]), wlat, row(g_q_lat[l]), row(g_kv_lat[l]), wq, wkv, wz, wxbc,
            conv_w[l].astype(F32), row(conv_b[l]), dtb, alog, inv128, tm=tm)
        attn = _attention(q, k, v, tq=tq)
        yg = _ssd(xs, bm, cm, dta, z, dsk, row(g_ssm_out[l]))
        x = _merge(x, attn, yg, mod8, row(g_pre_mix[l]), wga, wgb, w_o_attn[l].astype(BF16),
                   w_o_ssm[l].astype(BF16), w_out[l].astype(BF16), row(g_post_mix[l]), tm=tm)
        x = _mlp(x, mod8, row(g_pre_mlp[l]), w_ff1[l].astype(BF16), w_ff2[l].astype(BF16),
                 row(g_post_mlp[l]), tm=tm)
    return x
```

```python
import functools
import math

import jax
import jax.numpy as jnp
from jax import lax
from jax.experimental import pallas as pl
from jax.experimental.pallas import tpu as pltpu

F32 = jnp.float32
BF16 = jnp.bfloat16
HIGHEST = lax.Precision.HIGHEST

N_HEADS = 8
D_NOPE = 128
D_ROPE = 64
D_V = 128
D_QK = D_NOPE + D_ROPE
R_Q = 256
R_KV = 256
ROPE_THETA = 10000.0
SSM_INNER = 2048
SSM_P = 64
SSM_HEADS = 32
SSM_GROUPS = 8
SSM_N = 128
CONV_K = 4
CHUNK = 128
EPS = 1e-6
LOG2E = 1.4426950408889634

VMEM_LIMIT_BYTES = 56 * 1024 * 1024
LANES = 128

_NT = (((1,), (1,)), ((), ()))


def _dot(a, b):
    return jnp.dot(a, b, preferred_element_type=F32)


def _dot_nt(a, b, precision=None):
    return lax.dot_general(a, b, _NT, precision=precision, preferred_element_type=F32)


def _rms(x, g):
    ms = jnp.mean(x * x, axis=-1, keepdims=True)
    return x * lax.rsqrt(ms + EPS) * g


def _sigmoid(x):
    return 1.0 / (1.0 + jnp.exp(-x))


def _const_spec(shape):
    n = len(shape)
    return pl.BlockSpec(shape, lambda *_: (0,) * n, pipeline_mode=pl.Buffered(1))


def _ada_kernel(c_ref, w_ref, b_ref, o_ref):
    c = c_ref[...]
    sc = c * _sigmoid(c)
    o_ref[...] = jnp.dot(sc, w_ref[...], precision=HIGHEST, preferred_element_type=F32) + b_ref[...]


def _ada(c, w_ada, b_ada):
    bsz, d = c.shape
    n = w_ada.shape[1]
    blk = 1024
    return pl.pallas_call(
        _ada_kernel,
        grid=(n // blk,),
        in_specs=[pl.BlockSpec((bsz, d), lambda j: (0, 0)),
                  pl.BlockSpec((d, blk), lambda j: (0, j)),
                  pl.BlockSpec((1, blk), lambda j: (0, j))],
        out_specs=pl.BlockSpec((bsz, blk), lambda j: (0, j)),
        out_shape=jax.ShapeDtypeStruct((bsz, n), F32),
        name="ada",
    )(c, w_ada, b_ada.reshape(1, n))


def _front_kernel(x_ref, pos_ref, mod_ref, gpre_ref, wlat_ref, gq_ref, gkv_ref, wq_ref, wkv_ref,
                  wz_ref, wxbc_ref, convw_ref, convb_ref, dtb_ref, alog_ref, inv_ref,
                  q_ref, k_ref, v_ref, z_ref, xs_ref, bm_ref, cm_ref, dta_ref,
                  carry_ref, cbuf_ref, tin_ref, tout_ref, *, tm, cw, qscale, pitch):
    i = pl.program_id(1)

    @pl.when(i == 0)
    def _():
        carry_ref[...] = jnp.zeros_like(carry_ref)

    x = x_ref[...]
    shift1 = mod_ref[0:1, :]
    scale1 = mod_ref[1:2, :]
    h = _rms(x, gpre_ref[...]) * (1.0 + scale1) + shift1
    hb = h.astype(BF16)

    lat = _dot(hb, wlat_ref[...])
    qn = _rms(lat[:, 0:R_Q], gq_ref[...]).astype(BF16)
    kvn = _rms(lat[:, R_Q:R_Q + R_KV], gkv_ref[...]).astype(BF16)
    kr = lat[:, 512:640]
    dtr = lat[:, 640:768]

    lane = lax.broadcasted_iota(jnp.int32, (1, LANES), 1)
    pre = dtr + dtb_ref[...]
    sp = jnp.maximum(pre, 0.0) + jnp.log1p(jnp.exp(-jnp.abs(pre)))
    arow = jnp.where(lane < 32, 1.0, jnp.where(lane < 64, -jnp.exp(alog_ref[...]), 0.0))
    dta_ref[...] = sp * arow

    ang = jnp.tile(inv_ref[...], (1, tm // LANES)) * pos_ref[...].astype(F32)
    cs = jnp.cos(ang)
    sn = jnp.sin(ang)
    rot = jnp.concatenate([cs, cs, -sn, sn], axis=0).T

    def rope(t):
        u = t * rot
        return u + pltpu.roll(u, 64, axis=1)

    kro = rope(kr)[:, 0:D_ROPE].astype(BF16)

    for hd in range(N_HEADS):
        qh = _dot(qn, wq_ref[hd])
        q_ref[hd, :, 0:D_NOPE] = (qh[:, 0:D_NOPE] * qscale).astype(BF16)
        q_ref[hd, :, D_NOPE:D_QK] = (rope(qh[:, 128:256])[:, 0:D_ROPE] * qscale).astype(BF16)
        kvh = _dot(kvn, wkv_ref[hd])
        k_ref[hd, :, 0:D_NOPE] = kvh[:, 0:D_NOPE].astype(BF16)
        k_ref[hd, :, D_NOPE:D_QK] = kro
        v_ref[hd] = kvh[:, 128:256].astype(BF16)

    d = x.shape[1]
    rows_p = 8 * pitch
    for j in range(d // LANES):
        tin_ref[j, 0:tm, :] = h[:, j * LANES:(j + 1) * LANES]
        tin_ref[j, tm:rows_p, :] = jnp.zeros((rows_p - tm, LANES), F32)
    hp = jnp.concatenate(
        [jnp.concatenate([tin_ref[j, pl.ds(a, 8, stride=pitch), :] for j in range(d // LANES)], axis=1)
         for a in range(pitch)], axis=0).astype(BF16)
    sub = lax.broadcasted_iota(jnp.int32, (8, cw), 0)
    n_xbc = wxbc_ref.shape[1]
    z_every = n_xbc // SSM_INNER
    for ci, c0 in enumerate(range(0, n_xbc, cw)):
        slot = ci % 2
        cols = slice(c0, c0 + cw)
        u = _dot(hp, wxbc_ref[:, cols])
        if ci % z_every == 0:
            zc = slice((ci // z_every) * cw, (ci // z_every + 1) * cw)
            z_ref[:, zc] = _dot(hb, wz_ref[:, zc]).astype(BF16)
        for j in range(1, CONV_K):
            grp = pltpu.roll(u[(pitch - j) * 8:(pitch - j + 1) * 8, :], 1, axis=0)
            prev = carry_ref[CONV_K - 1 - j:CONV_K - j, cols]
            cbuf_ref[slot, (CONV_K - 1 - j) * 8:(CONV_K - j) * 8, :] = jnp.where(sub == 0, prev, grp)
        hdr = (CONV_K - 1) * 8
        cbuf_ref[slot, hdr:hdr + rows_p, :] = u
        for j in range(1, CONV_K):
            t = tm - j
            r = (t % pitch) * 8 + t // pitch
            carry_ref[CONV_K - 1 - j:CONV_K - j, cols] = u[r:r + 1, :]
        acc = convb_ref[:, cols]
        for kk in range(CONV_K):
            acc = acc + cbuf_ref[slot, kk * 8:kk * 8 + rows_p, :] * convw_ref[kk:kk + 1, cols]
        for a in range(pitch):
            for jj in range(cw // LANES):
                tout_ref[slot, jj, pl.ds(a, 8, stride=pitch), :] = acc[a * 8:(a + 1) * 8,
                                                                      jj * LANES:(jj + 1) * LANES]
        v = jnp.concatenate([tout_ref[slot, jj, 0:tm, :] for jj in range(cw // LANES)], axis=1)
        sv = (v * _sigmoid(v)).astype(BF16)
        if c0 < SSM_INNER:
            xs_ref[:, cols] = sv
        elif c0 < SSM_INNER + SSM_GROUPS * SSM_N:
            o0 = c0 - SSM_INNER
            bm_ref[:, o0:o0 + cw] = sv
        else:
            o0 = c0 - SSM_INNER - SSM_GROUPS * SSM_N
            cm_ref[:, o0:o0 + cw] = sv


def _front(x, pos3, mod8, gpre, wlat, gq, gkv, wq, wkv, wz, wxbc, convw, convb, dtb, alog, inv128, *, tm):
    bsz, seq, d = x.shape
    cw = 256
    assert tm % 64 == 0
    pitch = tm // 8 + 4
    qscale = float(D_QK ** -0.5 * LOG2E)
    n_bc = SSM_GROUPS * SSM_N
    tok = lambda width: pl.BlockSpec((None, tm, width), lambda b, i: (b, i, 0))
    head = lambda width: pl.BlockSpec((None, N_HEADS, tm, width), lambda b, i: (b, 0, i, 0))
    in_specs = [
        tok(d),
        pl.BlockSpec((None, 1, tm), lambda b, i: (b, 0, i)),
        pl.BlockSpec((None, 8, d), lambda b, i: (b, 0, 0)),
        _const_spec(gpre.shape), _const_spec(wlat.shape), _const_spec(gq.shape), _const_spec(gkv.shape),
        _const_spec(wq.shape), _const_spec(wkv.shape), _const_spec(wz.shape), _const_spec(wxbc.shape),
        _const_spec(convw.shape), _const_spec(convb.shape), _const_spec(dtb.shape), _const_spec(alog.shape),
        _const_spec(inv128.shape),
    ]
    out_specs = [head(D_QK), head(D_QK), head(D_V), tok(SSM_INNER), tok(SSM_INNER), tok(n_bc), tok(n_bc),
                 tok(LANES)]
    out_shape = [
        jax.ShapeDtypeStruct((bsz, N_HEADS, seq, D_QK), BF16),
        jax.ShapeDtypeStruct((bsz, N_HEADS, seq, D_QK), BF16),
        jax.ShapeDtypeStruct((bsz, N_HEADS, seq, D_V), BF16),
        jax.ShapeDtypeStruct((bsz, seq, SSM_INNER), BF16),
        jax.ShapeDtypeStruct((bsz, seq, SSM_INNER), BF16),
        jax.ShapeDtypeStruct((bsz, seq, n_bc), BF16),
        jax.ShapeDtypeStruct((bsz, seq, n_bc), BF16),
        jax.ShapeDtypeStruct((bsz, seq, LANES), F32),
    ]
    return pl.pallas_call(
        functools.partial(_front_kernel, tm=tm, cw=cw, qscale=qscale, pitch=pitch),
        grid=(bsz, seq // tm),
        in_specs=in_specs,
        out_specs=out_specs,
        out_shape=out_shape,
        scratch_shapes=[pltpu.VMEM((8, wxbc.shape[1]), F32),
                        pltpu.VMEM((2, (CONV_K - 1) * 8 + 8 * pitch, cw), F32),
                        pltpu.VMEM((d // LANES, 8 * pitch, LANES), F32),
                        pltpu.VMEM((2, cw // LANES, 8 * pitch, LANES), F32)],
        compiler_params=pltpu.CompilerParams(dimension_semantics=("parallel", "arbitrary"),
                                             vmem_limit_bytes=VMEM_LIMIT_BYTES),
        name="front",
    )(x, pos3, mod8, gpre, wlat, gq, gkv, wq, wkv, wz, wxbc, convw, convb, dtb, alog, inv128)


def _attn_kernel(q_ref, k_ref, v_ref, o_ref, sa_ref, sb_ref, m_ref, l_ref, acc_ref, *, tq, tk):
    i = pl.program_id(2)
    n_rep = tk // LANES

    def scores(rows, blk):
        return _dot_nt(q_ref[rows, :], k_ref[pl.ds(pl.multiple_of(blk * tk, tk), tk), :])

    def update(rows, blk, s):
        m_prev = m_ref[rows, :]
        m_new = jnp.maximum(m_prev, jnp.max(s, axis=-1, keepdims=True))
        alpha = jnp.exp2(m_prev - m_new)
        p = jnp.exp2(s - jnp.tile(m_new, (1, n_rep)))
        m_ref[rows, :] = m_new
        l_ref[rows, :] = alpha * l_ref[rows, :] + jnp.sum(p, axis=-1, keepdims=True)
        return alpha, p.astype(BF16), v_ref[pl.ds(pl.multiple_of(blk * tk, tk), tk), :]

    m_ref[...] = jnp.full(m_ref.shape, -jnp.inf, F32)
    l_ref[...] = jnp.zeros(l_ref.shape, F32)
    acc_ref[...] = jnp.zeros(acc_ref.shape, F32)
    allrows = slice(0, tq)
    bot = slice(tk, tq)
    sa_ref[...] = scores(allrows, 0)

    def step(blk, s_cur_ref, s_next_ref):
        alpha, p, vb = update(allrows, blk, s_cur_ref[...])
        s_next_ref[...] = scores(allrows, blk + 1)
        acc_ref[...] = alpha * acc_ref[...] + _dot(p, vb)

    def body(jj, _):
        step(2 * jj, sa_ref, sb_ref)
        step(2 * jj + 1, sb_ref, sa_ref)
        return 0

    lax.fori_loop(0, i, body, 0)

    row = lax.broadcasted_iota(jnp.int32, (tq, tk), 0)
    col = lax.broadcasted_iota(jnp.int32, (tq, tk), 1)
    alpha, p, vb = update(allrows, 2 * i, jnp.where(col <= row, sa_ref[...], -jnp.inf))
    s_bot = scores(bot, 2 * i + 1)
    acc_ref[...] = alpha * acc_ref[...] + _dot(p, vb)
    row_b = lax.broadcasted_iota(jnp.int32, (tk, tk), 0)
    col_b = lax.broadcasted_iota(jnp.int32, (tk, tk), 1)
    alpha, p, vb = update(bot, 2 * i + 1, jnp.where(col_b <= row_b, s_bot, -jnp.inf))
    acc_ref[bot, :] = alpha * acc_ref[bot, :] + _dot(p, vb)
    o_ref[...] = (acc_ref[...] / l_ref[...]).astype(BF16)


def _attention(q, k, v, *, tq):
    bsz, nh, seq, _ = q.shape
    return pl.pallas_call(
        functools.partial(_attn_kernel, tq=tq, tk=tq // 2),
        grid=(bsz, nh, seq // tq),
        in_specs=[pl.BlockSpec((None, None, tq, D_QK), lambda b, h, i: (b, h, i, 0)),
                  pl.BlockSpec((None, None, seq, D_QK), lambda b, h, i: (b, h, 0, 0)),
                  pl.BlockSpec((None, None, seq, D_V), lambda b, h, i: (b, h, 0, 0))],
        out_specs=pl.BlockSpec((None, tq, D_V), lambda b, h, i: (b, i, h)),
        out_shape=jax.ShapeDtypeStruct((bsz, seq, nh * D_V), BF16),
        scratch_shapes=[pltpu.VMEM((tq, tq // 2), F32), pltpu.VMEM((tq, tq // 2), F32), pltpu.VMEM((tq, LANES), F32),
                        pltpu.VMEM((tq, LANES), F32), pltpu.VMEM((tq, D_V), F32)],
        compiler_params=pltpu.CompilerParams(dimension_semantics=("parallel", "parallel", "arbitrary"),
                                             vmem_limit_bytes=VMEM_LIMIT_BYTES),
        name="attn",
    )(q, k, v)


def _ssd_kernel(xs_ref, bm_ref, cm_ref, dta_ref, z_ref, dsk_ref, gss_ref, o_ref,
                state_ref, rows_ref):
    ci = pl.program_id(1)

    @pl.when(ci == 0)
    def _():
        state_ref[...] = jnp.zeros_like(state_ref)

    L = CHUNK
    r_i = lax.broadcasted_iota(jnp.int32, (L, L), 0)
    c_i = lax.broadcasted_iota(jnp.int32, (L, L), 1)
    tril = c_i <= r_i
    eye_b = (r_i == c_i).astype(BF16)
    lane = lax.broadcasted_iota(jnp.int32, (1, L), 1)

    dta = dta_ref[...]
    cum = jnp.dot(tril.astype(F32), dta, precision=HIGHEST, preferred_element_type=F32)
    in_a = (lane >= 32) & (lane < 64)
    acm2 = jnp.where(in_a, cum * LOG2E, 0.0)
    comb_t = jnp.where(in_a, acm2, dta).T
    dt_t = comb_t[0:SSM_HEADS]
    a2_t = comb_t[SSM_HEADS:2 * SSM_HEADS]
    rows_ref[0] = a2_t - jnp.log2(dt_t)
    rows_ref[1] = jnp.exp2(a2_t[:, L - 1:L] - a2_t) * dt_t
    cd_all = jnp.exp2(acm2[L - 1:L, :])

    for g in range(SSM_GROUPS):
        cg = cm_ref[:, g * SSM_N:(g + 1) * SSM_N]
        bg = bm_ref[:, g * SSM_N:(g + 1) * SSM_N]
        cb = _dot_nt(cg, bg)
        bt = _dot_nt(eye_b, bg)
        cg32 = cg.astype(F32)
        ys = []
        for pr in range(2):
            hp = g * 2 + pr
            xpair = xs_ref[:, hp * 128:(hp + 1) * 128]
            s_old = state_ref[hp]
            rhs = jnp.concatenate([xpair, s_old.astype(BF16)], axis=0)
            lhs_parts = []
            bw_parts = []
            for e in range(2):
                hh = hp * 2 + e
                colb = jnp.broadcast_to(acm2[:, 32 + hh:33 + hh], (L, L))
                dec = jnp.exp2(jnp.where(tril, colb - rows_ref[0, hh:hh + 1, :], -jnp.inf))
                mh = (cb * dec).astype(BF16)
                ce = (cg32 * jnp.exp2(colb)).astype(BF16)
                lhs_parts.append(jnp.concatenate([mh, ce], axis=1))
                bw_parts.append((bt * rows_ref[1, hh:hh + 1, :]).astype(BF16))
            yy = _dot(jnp.concatenate(lhs_parts, axis=0), rhs)
            ss = _dot(jnp.concatenate(bw_parts, axis=0), xpair)
            first = lane < 64
            ys.append(jnp.where(first, yy[0:L], yy[L:2 * L]))
            h0 = hp * 2
            cd0 = cd_all[:, 32 + h0:33 + h0]
            cd1 = cd_all[:, 33 + h0:34 + h0]
            cdrow = jnp.where(first, cd0, cd1)
            state_ref[hp] = cdrow * s_old + jnp.where(first, ss[0:SSM_N], ss[SSM_N:2 * SSM_N])
        sl = slice(g * 256, (g + 1) * 256)
        xg = xs_ref[:, sl].astype(F32)
        y = jnp.concatenate(ys, axis=1) + dsk_ref[:, sl] * xg
        zg = z_ref[:, sl].astype(F32)
        yz = y * (zg * _sigmoid(zg))
        o_ref[:, sl] = _rms(yz, gss_ref[:, sl]).astype(BF16)


def _ssd(xs, bm, cm, dta, z, dsk, gss):
    bsz, seq, _ = xs.shape
    n_bc = SSM_GROUPS * SSM_N
    tok = lambda width: pl.BlockSpec((None, CHUNK, width), lambda b, c: (b, c, 0))
    return pl.pallas_call(
        _ssd_kernel,
        grid=(bsz, seq // CHUNK),
        in_specs=[tok(SSM_INNER), tok(n_bc), tok(n_bc), tok(LANES), tok(SSM_INNER),
                  _const_spec(dsk.shape), _const_spec(gss.shape)],
        out_specs=tok(SSM_INNER),
        out_shape=jax.ShapeDtypeStruct((bsz, seq, SSM_INNER), BF16),
        scratch_shapes=[pltpu.VMEM((SSM_HEADS // 2, SSM_N, 128), F32), pltpu.VMEM((2, SSM_HEADS, CHUNK), F32)],
        compiler_params=pltpu.CompilerParams(dimension_semantics=("parallel", "arbitrary"),
                                             vmem_limit_bytes=VMEM_LIMIT_BYTES),
        name="ssd",
    )(xs, bm, cm, dta, z, dsk, gss)


def _merge_kernel(x_ref, attn_ref, yg_ref, mod_ref, gpre_ref, wga_ref, wgb_ref, woa_ref, wob_ref,
                  wout_ref, gpost_ref, o_ref, mb_ref, *, cw):
    x = x_ref[...]
    shift1 = mod_ref[0:1, :]
    scale1 = mod_ref[1:2, :]
    gate1 = mod_ref[2:3, :]
    hb = (_rms(x, gpre_ref[...]) * (1.0 + scale1) + shift1).astype(BF16)
    at = attn_ref[...]
    yg = yg_ref[...]
    d = x.shape[1]
    for c0 in range(0, d, cw):
        sl = slice(c0, c0 + cw)
        ga = _sigmoid(_dot(hb, wga_ref[:, sl]))
        gb = _sigmoid(_dot(hb, wgb_ref[:, sl]))
        merged = ga * _dot(at, woa_ref[:, sl]) + gb * _dot(yg, wob_ref[:, sl])
        mb_ref[:, sl] = merged.astype(BF16)
    mix = _dot(mb_ref[...], wout_ref[...])
    o_ref[...] = x + gate1 * _rms(mix, gpost_ref[...])


def _merge(x, attn, yg, mod8, gpre, wga, wgb, woa, wob, wout, gpost, *, tm):
    bsz, seq, d = x.shape
    tok = lambda width: pl.BlockSpec((None, tm, width), lambda b, i: (b, i, 0))
    return pl.pallas_call(
        functools.partial(_merge_kernel, cw=512),
        grid=(bsz, seq // tm),
        in_specs=[tok(d), tok(attn.shape[2]), tok(yg.shape[2]),
                  pl.BlockSpec((None, 8, d), lambda b, i: (b, 0, 0)),
                  _const_spec(gpre.shape), _const_spec(wga.shape), _const_spec(wgb.shape),
                  _const_spec(woa.shape), _const_spec(wob.shape), _const_spec(wout.shape),
                  _const_spec(gpost.shape)],
        out_specs=tok(d),
        out_shape=jax.ShapeDtypeStruct((bsz, seq, d), F32),
        scratch_shapes=[pltpu.VMEM((tm, d), BF16)],
        compiler_params=pltpu.CompilerParams(dimension_semantics=("parallel", "parallel"),
                                             vmem_limit_bytes=VMEM_LIMIT_BYTES),
        name="merge",
    )(x, attn, yg, mod8, gpre, wga, wgb, woa, wob, wout, gpost)


def _mlp_kernel(x_ref, mod_ref, gpre_ref, w1_ref, w2_ref, gpost_ref, o_ref, acc_ref, *, cw):
    x = x_ref[...]
    shift2 = mod_ref[3:4, :]
    scale2 = mod_ref[4:5, :]
    gate2 = mod_ref[5:6, :]
    hb = (_rms(x, gpre_ref[...]) * (1.0 + scale2) + shift2).astype(BF16)
    dff = w1_ref.shape[1]
    for ci, c0 in enumerate(range(0, dff, cw)):
        a = jnp.maximum(_dot(hb, w1_ref[:, c0:c0 + cw]), 0.0)
        part = _dot((a * a).astype(BF16), w2_ref[c0:c0 + cw, :])
        if ci == 0:
            acc_ref[...] = part
        else:
            acc_ref[...] += part
    o_ref[...] = x + gate2 * _rms(acc_ref[...], gpost_ref[...])


def _mlp(x, mod8, gpre, w1, w2, gpost, *, tm):
    bsz, seq, d = x.shape
    tok = lambda width: pl.BlockSpec((None, tm, width), lambda b, i: (b, i, 0))
    return pl.pallas_call(
        functools.partial(_mlp_kernel, cw=1024),
        grid=(bsz, seq // tm),
        in_specs=[tok(d), pl.BlockSpec((None, 8, d), lambda b, i: (b, 0, 0)),
                  _const_spec(gpre.shape), _const_spec(w1.shape), _const_spec(w2.shape),
                  _const_spec(gpost.shape)],
        out_specs=tok(d),
        out_shape=jax.ShapeDtypeStruct((bsz, seq, d), F32),
        scratch_shapes=[pltpu.VMEM((tm, d), F32)],
        compiler_params=pltpu.CompilerParams(dimension_semantics=("parallel", "parallel"),
                                             vmem_limit_bytes=VMEM_LIMIT_BYTES),
        name="mlp",
    )(x, mod8, gpre, w1, w2, gpost)


def _pad_row(v, width, offset=0):
    out = jnp.zeros((1, width), F32)
    return out.at[0, offset:offset + v.shape[0]].set(v.astype(F32))


def kernel(x, c, positions, w_ada, b_ada, g_pre_mix, g_post_mix, w_in, g_q_lat, g_kv_lat, w_uq, w_ukv,
           w_o_attn, conv_w, conv_b, dt_bias, a_log, d_skip, g_ssm_out, w_o_ssm, w_out, g_pre_mlp,
           g_post_mlp, w_ff1, w_ff2):
    bsz, seq, d = x.shape
    depth = w_ada.shape[0]
    tm = min(512, seq)
    tq = min(1024, seq)
    half = D_ROPE // 2
    inv = ROPE_THETA ** (-jnp.arange(half, dtype=F32) / half)
    inv128 = jnp.broadcast_to(inv[:, None], (half, LANES))
    pos3 = positions.reshape(bsz, 1, seq)
    row = lambda v: v.reshape(1, -1).astype(F32)

    for l in range(depth):
        wi = w_in[l]
        o_q, o_kv, o_kr, o_z = 0, R_Q, R_Q + R_KV, R_Q + R_KV + D_ROPE
        o_xbc = o_z + SSM_INNER
        o_dt = o_xbc + SSM_INNER + 2 * SSM_GROUPS * SSM_N
        o_ga = o_dt + SSM_HEADS
        o_gb = o_ga + d
        kr1 = wi[:, o_kr:o_kr + half]
        kr2 = wi[:, o_kr + half:o_kr + D_ROPE]
        wdt = wi[:, o_dt:o_dt + SSM_HEADS]
        wlat = jnp.concatenate(
            [wi[:, o_q:o_kr], kr1, kr2, kr2, kr1, wdt, wdt, jnp.zeros((d, 64), F32)], axis=1).astype(BF16)
        wz = wi[:, o_z:o_xbc].astype(BF16)
        wxbc = wi[:, o_xbc:o_dt].astype(BF16)
        wga = wi[:, o_ga:o_gb].astype(BF16)
        wgb = wi[:, o_gb:o_gb + d].astype(BF16)
        wq3 = w_uq[l].reshape(R_Q, N_HEADS, D_QK)
        q1 = wq3[:, :, D_NOPE:D_NOPE + half]
        q2 = wq3[:, :, D_NOPE + half:]
        wq = jnp.concatenate([wq3[:, :, :D_NOPE], q1, q2, q2, q1], axis=2).transpose(1, 0, 2).astype(BF16)
        wkv = w_ukv[l].reshape(R_KV, N_HEADS, D_NOPE + D_V).transpose(1, 0, 2).astype(BF16)
        dtb = jnp.concatenate([row(dt_bias[l]), row(dt_bias[l]), jnp.zeros((1, 64), F32)], axis=1)
        alog = _pad_row(a_log[l], LANES, 32)
        dsk = jnp.repeat(d_skip[l].astype(F32), SSM_P).reshape(1, SSM_INNER)

        mod = _ada(c, w_ada[l], b_ada[l])
        mod8 = jnp.concatenate([mod.reshape(bsz, 6, d), jnp.zeros((bsz, 2, d), F32)], axis=1)

        q, k, v, z, xs, bm, cm, dta = _front(
            x, pos3, mod8, row(g_pre_mix[l]), wlat, row(g_q_lat[l]), row(g_kv_lat[l]), wq, wkv, wz, wxbc,
            conv_w[l].astype(F32), row(conv_b[l]), dtb, alog, inv128, tm=tm)
        attn = _attention(q, k, v, tq=tq)
        yg = _ssd(xs, bm, cm, dta, z, dsk, row(g_ssm_out[l]))
        x = _merge(x, attn, yg, mod8, row(g_pre_mix[l]), wga, wgb, w_o_attn[l].astype(BF16),
                   w_o_ssm[l].astype(BF16), w_out[l].astype(BF16), row(g_post_mix[l]), tm=tm)
        x = _mlp(x, mod8, row(g_pre_mlp[l]), w_ff1[l].astype(BF16), w_ff2[l].astype(BF16),
                 row(g_post_mlp[l]), tm=tm)
    return x
```

```python
import functools
import math

import jax
import jax.numpy as jnp
from jax import lax
from jax.experimental import pallas as pl
from jax.experimental.pallas import tpu as pltpu

F32 = jnp.float32
BF16 = jnp.bfloat16
HIGHEST = lax.Precision.HIGHEST

N_HEADS = 8
D_NOPE = 128
D_ROPE = 64
D_V = 128
D_QK = D_NOPE + D_ROPE
R_Q = 256
R_KV = 256
ROPE_THETA = 10000.0
SSM_INNER = 2048
SSM_P = 64
SSM_HEADS = 32
SSM_GROUPS = 8
SSM_N = 128
CONV_K = 4
CHUNK = 128
EPS = 1e-6
LOG2E = 1.4426950408889634

VMEM_LIMIT_BYTES = 56 * 1024 * 1024
LANES = 128

_NT = (((1,), (1,)), ((), ()))


def _dot(a, b):
    return jnp.dot(a, b, preferred_element_type=F32)


def _dot_nt(a, b, precision=None):
    return lax.dot_general(a, b, _NT, precision=precision, preferred_element_type=F32)


def _rms(x, g):
    ms = jnp.mean(x * x, axis=-1, keepdims=True)
    return x * lax.rsqrt(ms + EPS) * g


def _sigmoid(x):
    return 1.0 / (1.0 + jnp.exp(-x))


def _const_spec(shape):
    n = len(shape)
    return pl.BlockSpec(shape, lambda *_: (0,) * n, pipeline_mode=pl.Buffered(1))


def _ada_kernel(c_ref, w_ref, b_ref, o_ref):
    c = c_ref[...]
    sc = c * _sigmoid(c)
    o_ref[...] = jnp.dot(sc, w_ref[...], precision=HIGHEST, preferred_element_type=F32) + b_ref[...]


def _ada(c, w_ada, b_ada):
    bsz, d = c.shape
    n = w_ada.shape[1]
    blk = 1024
    return pl.pallas_call(
        _ada_kernel,
        grid=(n // blk,),
        in_specs=[pl.BlockSpec((bsz, d), lambda j: (0, 0)),
                  pl.BlockSpec((d, blk), lambda j: (0, j)),
                  pl.BlockSpec((1, blk), lambda j: (0, j))],
        out_specs=pl.BlockSpec((bsz, blk), lambda j: (0, j)),
        out_shape=jax.ShapeDtypeStruct((bsz, n), F32),
        name="ada",
    )(c, w_ada, b_ada.reshape(1, n))


def _front_kernel(x_ref, pos_ref, mod_ref, gpre_ref, wlat_ref, gq_ref, gkv_ref, wq_ref, wkv_ref,
                  wz_ref, wxbc_ref, convw_ref, convb_ref, dtb_ref, alog_ref, inv_ref,
                  q_ref, k_ref, v_ref, z_ref, xs_ref, bm_ref, cm_ref, dta_ref,
                  carry_ref, cbuf_ref, tin_ref, tout_ref, *, tm, cw, qscale, pitch):
    i = pl.program_id(1)

    @pl.when(i == 0)
    def _():
        carry_ref[...] = jnp.zeros_like(carry_ref)

    x = x_ref[...]
    shift1 = mod_ref[0:1, :]
    scale1 = mod_ref[1:2, :]
    h = _rms(x, gpre_ref[...]) * (1.0 + scale1) + shift1
    hb = h.astype(BF16)

    lat = _dot(hb, wlat_ref[...])
    qn = _rms(lat[:, 0:R_Q], gq_ref[...]).astype(BF16)
    kvn = _rms(lat[:, R_Q:R_Q + R_KV], gkv_ref[...]).astype(BF16)
    kr = lat[:, 512:640]
    dtr = lat[:, 640:768]

    lane = lax.broadcasted_iota(jnp.int32, (1, LANES), 1)
    pre = dtr + dtb_ref[...]
    sp = jnp.maximum(pre, 0.0) + jnp.log1p(jnp.exp(-jnp.abs(pre)))
    arow = jnp.where(lane < 32, 1.0, jnp.where(lane < 64, -jnp.exp(alog_ref[...]), 0.0))
    dta_ref[...] = sp * arow

    ang = jnp.tile(inv_ref[...], (1, tm // LANES)) * pos_ref[...].astype(F32)
    cs = jnp.cos(ang)
    sn = jnp.sin(ang)
    rot = jnp.concatenate([cs, cs, -sn, sn], axis=0).T

    def rope(t):
        u = t * rot
        return u + pltpu.roll(u, 64, axis=1)

    kro = rope(kr)[:, 0:D_ROPE].astype(BF16)

    for hd in range(N_HEADS):
        qh = _dot(qn, wq_ref[hd])
        q_ref[hd, :, 0:D_NOPE] = (qh[:, 0:D_NOPE] * qscale).astype(BF16)
        q_ref[hd, :, D_NOPE:D_QK] = (rope(qh[:, 128:256])[:, 0:D_ROPE] * qscale).astype(BF16)
        kvh = _dot(kvn, wkv_ref[hd])
        k_ref[hd, :, 0:D_NOPE] = kvh[:, 0:D_NOPE].astype(BF16)
        k_ref[hd, :, D_NOPE:D_QK] = kro
        v_ref[hd] = kvh[:, 128:256].astype(BF16)

    d = x.shape[1]
    rows_p = 8 * pitch
    for j in range(d // LANES):
        tin_ref[j, 0:tm, :] = h[:, j * LANES:(j + 1) * LANES]
        tin_ref[j, tm:rows_p, :] = jnp.zeros((rows_p - tm, LANES), F32)
    hp = jnp.concatenate(
        [jnp.concatenate([tin_ref[j, pl.ds(a, 8, stride=pitch), :] for j in range(d // LANES)], axis=1)
         for a in range(pitch)], axis=0).astype(BF16)
    sub = lax.broadcasted_iota(jnp.int32, (8, cw), 0)
    n_xbc = wxbc_ref.shape[1]
    z_every = n_xbc // SSM_INNER
    for ci, c0 in enumerate(range(0, n_xbc, cw)):
        slot = ci % 2
        cols = slice(c0, c0 + cw)
        u = _dot(hp, wxbc_ref[:, cols])
        if ci % z_every == 0:
            zc = slice((ci // z_every) * cw, (ci // z_every + 1) * cw)
            z_ref[:, zc] = _dot(hb, wz_ref[:, zc]).astype(BF16)
        for j in range(1, CONV_K):
            grp = pltpu.roll(u[(pitch - j) * 8:(pitch - j + 1) * 8, :], 1, axis=0)
            prev = carry_ref[CONV_K - 1 - j:CONV_K - j, cols]
            cbuf_ref[slot, (CONV_K - 1 - j) * 8:(CONV_K - j) * 8, :] = jnp.where(sub == 0, prev, grp)
        hdr = (CONV_K - 1) * 8
        cbuf_ref[slot, hdr:hdr + rows_p, :] = u
        for j in range(1, CONV_K):
            t = tm - j
            r = (t % pitch) * 8 + t // pitch
            carry_ref[CONV_K - 1 - j:CONV_K - j, cols] = u[r:r + 1, :]
        acc = convb_ref[:, cols]
        for kk in range(CONV_K):
            acc = acc + cbuf_ref[slot, kk * 8:kk * 8 + rows_p, :] * convw_ref[kk:kk + 1, cols]
        for a in range(pitch):
            for jj in range(cw // LANES):
                tout_ref[slot, jj, pl.ds(a, 8, stride=pitch), :] = acc[a * 8:(a + 1) * 8,
                                                                      jj * LANES:(jj + 1) * LANES]
        v = jnp.concatenate([tout_ref[slot, jj, 0:tm, :] for jj in range(cw // LANES)], axis=1)
        sv = (v * _sigmoid(v)).astype(BF16)
        if c0 < SSM_INNER:
            xs_ref[:, cols] = sv
        elif c0 < SSM_INNER + SSM_GROUPS * SSM_N:
            o0 = c0 - SSM_INNER
            bm_ref[:, o0:o0 + cw] = sv
        else:
            o0 = c0 - SSM_INNER - SSM_GROUPS * SSM_N
            cm_ref[:, o0:o0 + cw] = sv


def _front(x, pos3, mod8, gpre, wlat, gq, gkv, wq, wkv, wz, wxbc, convw, convb, dtb, alog, inv128, *, tm):
    bsz, seq, d = x.shape
    cw = 256
    assert tm % 64 == 0
    pitch = tm // 8 + 4
    qscale = float(D_QK ** -0.5 * LOG2E)
    n_bc = SSM_GROUPS * SSM_N
    tok = lambda width: pl.BlockSpec((None, tm, width), lambda b, i: (b, i, 0))
    head = lambda width: pl.BlockSpec((None, N_HEADS, tm, width), lambda b, i: (b, 0, i, 0))
    in_specs = [
        tok(d),
        pl.BlockSpec((None, 1, tm), lambda b, i: (b, 0, i)),
        pl.BlockSpec((None, 8, d), lambda b, i: (b, 0, 0)),
        _const_spec(gpre.shape), _const_spec(wlat.shape), _const_spec(gq.shape), _const_spec(gkv.shape),
        _const_spec(wq.shape), _const_spec(wkv.shape), _const_spec(wz.shape), _const_spec(wxbc.shape),
        _const_spec(convw.shape), _const_spec(convb.shape), _const_spec(dtb.shape), _const_spec(alog.shape),
        _const_spec(inv128.shape),
    ]
    out_specs = [head(D_QK), head(D_QK), head(D_V), tok(SSM_INNER), tok(SSM_INNER), tok(n_bc), tok(n_bc),
                 tok(LANES)]
    out_shape = [
        jax.ShapeDtypeStruct((bsz, N_HEADS, seq, D_QK), BF16),
        jax.ShapeDtypeStruct((bsz, N_HEADS, seq, D_QK), BF16),
        jax.ShapeDtypeStruct((bsz, N_HEADS, seq, D_V), BF16),
        jax.ShapeDtypeStruct((bsz, seq, SSM_INNER), BF16),
        jax.ShapeDtypeStruct((bsz, seq, SSM_INNER), BF16),
        jax.ShapeDtypeStruct((bsz, seq, n_bc), BF16),
        jax.ShapeDtypeStruct((bsz, seq, n_bc), BF16),
        jax.ShapeDtypeStruct((bsz, seq, LANES), F32),
    ]
    return pl.pallas_call(
        functools.partial(_front_kernel, tm=tm, cw=cw, qscale=qscale, pitch=pitch),
        grid=(bsz, seq // tm),
        in_specs=in_specs,
        out_specs=out_specs,
        out_shape=out_shape,
        scratch_shapes=[pltpu.VMEM((8, wxbc.shape[1]), F32),
                        pltpu.VMEM((2, (CONV_K - 1) * 8 + 8 * pitch, cw), F32),
                        pltpu.VMEM((d // LANES, 8 * pitch, LANES), F32),
                        pltpu.VMEM((2, cw // LANES, 8 * pitch, LANES), F32)],
        compiler_params=pltpu.CompilerParams(dimension_semantics=("parallel", "arbitrary"),
                                             vmem_limit_bytes=VMEM_LIMIT_BYTES),
        name="front",
    )(x, pos3, mod8, gpre, wlat, gq, gkv, wq, wkv, wz, wxbc, convw, convb, dtb, alog, inv128)


def _attn_kernel(q_ref, k_ref, v_ref, o_ref, sa_ref, sb_ref, m_ref, l_ref, acc_ref, *, tq, tk, n_tiles):
    n_rep = tk // LANES
    bufs = (sa_ref, sb_ref)
    row = lax.broadcasted_iota(jnp.int32, (tq, tk), 0)
    col = lax.broadcasted_iota(jnp.int32, (tq, tk), 1)
    row_b = lax.broadcasted_iota(jnp.int32, (tk, tk), 0)
    col_b = lax.broadcasted_iota(jnp.int32, (tk, tk), 1)

    def scores(r0, n_rows, blk):
        return _dot_nt(q_ref[r0:r0 + n_rows, :], k_ref[blk * tk:(blk + 1) * tk, :])

    def update(par, rows, blk, s):
        m_prev = m_ref[par, rows, :]
        m_new = jnp.maximum(m_prev, jnp.max(s, axis=-1, keepdims=True))
        alpha = jnp.exp2(m_prev - m_new)
        p = jnp.exp2(s - jnp.tile(m_new, (1, n_rep)))
        m_ref[par, rows, :] = m_new
        l_ref[par, rows, :] = alpha * l_ref[par, rows, :] + jnp.sum(p, axis=-1, keepdims=True)
        pv = _dot(p.astype(BF16), v_ref[blk * tk:(blk + 1) * tk, :])
        return alpha, pv

    allrows = slice(0, tq)
    bot = slice(tk, tq)
    cur = 0
    bufs[cur][...] = scores(0, tq, 0)
    for i in range(n_tiles):
        par = i % 2
        r0 = i * tq
        m_ref[par] = jnp.full((tq, LANES), -jnp.inf, F32)
        l_ref[par] = jnp.zeros((tq, LANES), F32)
        acc_ref[par] = jnp.zeros((tq, D_V), F32)
        for blk in range(2 * i):
            alpha, pv = update(par, allrows, blk, bufs[cur][...])
            bufs[1 - cur][...] = scores(r0, tq, blk + 1)
            acc_ref[par] = alpha * acc_ref[par] + pv
            cur = 1 - cur
        alpha, pv = update(par, allrows, 2 * i, jnp.where(col <= row, bufs[cur][...], -jnp.inf))
        bufs[1 - cur][0:tk, :] = scores(r0 + tk, tk, 2 * i + 1)
        acc_ref[par] = alpha * acc_ref[par] + pv
        alpha, pv = update(par, bot, 2 * i + 1, jnp.where(col_b <= row_b, bufs[1 - cur][0:tk, :], -jnp.inf))
        if i + 1 < n_tiles:
            bufs[cur][...] = scores(r0 + tq, tq, 0)
        acc_ref[par, bot, :] = alpha * acc_ref[par, bot, :] + pv
        o_ref[r0:r0 + tq, :] = (acc_ref[par] / l_ref[par]).astype(BF16)


def _attention(q, k, v, *, tq):
    bsz, nh, seq, _ = q.shape
    whole = lambda width: pl.BlockSpec((None, None, seq, width), lambda b, h: (b, h, 0, 0))
    return pl.pallas_call(
        functools.partial(_attn_kernel, tq=tq, tk=tq // 2, n_tiles=seq // tq),
        grid=(bsz, nh),
        in_specs=[whole(D_QK), whole(D_QK), whole(D_V)],
        out_specs=pl.BlockSpec((None, seq, D_V), lambda b, h: (b, 0, h)),
        out_shape=jax.ShapeDtypeStruct((bsz, seq, nh * D_V), BF16),
        scratch_shapes=[pltpu.VMEM((tq, tq // 2), F32), pltpu.VMEM((tq, tq // 2), F32),
                        pltpu.VMEM((2, tq, LANES), F32), pltpu.VMEM((2, tq, LANES), F32),
                        pltpu.VMEM((2, tq, D_V), F32)],
        compiler_params=pltpu.CompilerParams(dimension_semantics=("parallel", "parallel"),
                                             vmem_limit_bytes=VMEM_LIMIT_BYTES),
        name="attn",
    )(q, k, v)


def _ssd_kernel(xs_ref, bm_ref, cm_ref, dta_ref, z_ref, dsk_ref, gss_ref, o_ref,
                state_ref, rows_ref):
    ci = pl.program_id(1)

    @pl.when(ci == 0)
    def _():
        state_ref[...] = jnp.zeros_like(state_ref)

    L = CHUNK
    r_i = lax.broadcasted_iota(jnp.int32, (L, L), 0)
    c_i = lax.broadcasted_iota(jnp.int32, (L, L), 1)
    tril = c_i <= r_i
    eye_b = (r_i == c_i).astype(BF16)
    lane = lax.broadcasted_iota(jnp.int32, (1, L), 1)

    dta = dta_ref[...]
    cum = jnp.dot(tril.astype(F32), dta, precision=HIGHEST, preferred_element_type=F32)
    in_a = (lane >= 32) & (lane < 64)
    acm2 = jnp.where(in_a, cum * LOG2E, 0.0)
    comb_t = jnp.where(in_a, acm2, dta).T
    dt_t = comb_t[0:SSM_HEADS]
    a2_t = comb_t[SSM_HEADS:2 * SSM_HEADS]
    rows_ref[0] = a2_t - jnp.log2(dt_t)
    rows_ref[1] = jnp.exp2(a2_t[:, L - 1:L] - a2_t) * dt_t
    cd_all = jnp.exp2(acm2[L - 1:L, :])

    for g in range(SSM_GROUPS):
        cg = cm_ref[:, g * SSM_N:(g + 1) * SSM_N]
        bg = bm_ref[:, g * SSM_N:(g + 1) * SSM_N]
        cb = _dot_nt(cg, bg)
        bt = _dot_nt(eye_b, bg)
        cg32 = cg.astype(F32)
        ys = []
        for pr in range(2):
            hp = g * 2 + pr
            xpair = xs_ref[:, hp * 128:(hp + 1) * 128]
            s_old = state_ref[hp]
            rhs = jnp.concatenate([xpair, s_old.astype(BF16)], axis=0)
            lhs_parts = []
            bw_parts = []
            for e in range(2):
                hh = hp * 2 + e
                colb = jnp.broadcast_to(acm2[:, 32 + hh:33 + hh], (L, L))
                dec = jnp.exp2(jnp.where(tril, colb - rows_ref[0, hh:hh + 1, :], -jnp.inf))
                mh = (cb * dec).astype(BF16)
                ce = (cg32 * jnp.exp2(colb)).astype(BF16)
                lhs_parts.append(jnp.concatenate([mh, ce], axis=1))
                bw_parts.append((bt * rows_ref[1, hh:hh + 1, :]).astype(BF16))
            yy = _dot(jnp.concatenate(lhs_parts, axis=0), rhs)
            ss = _dot(jnp.concatenate(bw_parts, axis=0), xpair)
            first = lane < 64
            ys.append(jnp.where(first, yy[0:L], yy[L:2 * L]))
            h0 = hp * 2
            cd0 = cd_all[:, 32 + h0:33 + h0]
            cd1 = cd_all[:, 33 + h0:34 + h0]
            cdrow = jnp.where(first, cd0, cd1)
            state_ref[hp] = cdrow * s_old + jnp.where(first, ss[0:SSM_N], ss[SSM_N:2 * SSM_N])
        sl = slice(g * 256, (g + 1) * 256)
        xg = xs_ref[:, sl].astype(F32)
        y = jnp.concatenate(ys, axis=1) + dsk_ref[:, sl] * xg
        zg = z_ref[:, sl].astype(F32)
        yz = y * (zg * _sigmoid(zg))
        o_ref[:, sl] = _rms(yz, gss_ref[:, sl]).astype(BF16)


def _ssd(xs, bm, cm, dta, z, dsk, gss):
    bsz, seq, _ = xs.shape
    n_bc = SSM_GROUPS * SSM_N
    tok = lambda width: pl.BlockSpec((None, CHUNK, width), lambda b, c: (b, c, 0))
    return pl.pallas_call(
        _ssd_kernel,
        grid=(bsz, seq // CHUNK),
        in_specs=[tok(SSM_INNER), tok(n_bc), tok(n_bc), tok(LANES), tok(SSM_INNER),
                  _const_spec(dsk.shape), _const_spec(gss.shape)],
        out_specs=tok(SSM_INNER),
        out_shape=jax.ShapeDtypeStruct((bsz, seq, SSM_INNER), BF16),
        scratch_shapes=[pltpu.VMEM((SSM_HEADS // 2, SSM_N, 128), F32), pltpu.VMEM((2, SSM_HEADS, CHUNK), F32)],
        compiler_params=pltpu.CompilerParams(dimension_semantics=("parallel", "arbitrary"),
                                             vmem_limit_bytes=VMEM_LIMIT_BYTES),
        name="ssd",
    )(xs, bm, cm, dta, z, dsk, gss)


def _merge_kernel(x_ref, attn_ref, yg_ref, mod_ref, gpre_ref, wga_ref, wgb_ref, woa_ref, wob_ref,
                  wout_ref, gpost_ref, o_ref, mb_ref, *, cw):
    x = x_ref[...]
    shift1 = mod_ref[0:1, :]
    scale1 = mod_ref[1:2, :]
    gate1 = mod_ref[2:3, :]
    hb = (_rms(x, gpre_ref[...]) * (1.0 + scale1) + shift1).astype(BF16)
    at = attn_ref[...]
    yg = yg_ref[...]
    d = x.shape[1]
    for c0 in range(0, d, cw):
        sl = slice(c0, c0 + cw)
        ga = _sigmoid(_dot(hb, wga_ref[:, sl]))
        gb = _sigmoid(_dot(hb, wgb_ref[:, sl]))
        merged = ga * _dot(at, woa_ref[:, sl]) + gb * _dot(yg, wob_ref[:, sl])
        mb_ref[:, sl] = merged.astype(BF16)
    mix = _dot(mb_ref[...], wout_ref[...])
    o_ref[...] = x + gate1 * _rms(mix, gpost_ref[...])


def _merge(x, attn, yg, mod8, gpre, wga, wgb, woa, wob, wout, gpost, *, tm):
    bsz, seq, d = x.shape
    tok = lambda width: pl.BlockSpec((None, tm, width), lambda b, i: (b, i, 0))
    return pl.pallas_call(
        functools.partial(_merge_kernel, cw=512),
        grid=(bsz, seq // tm),
        in_specs=[tok(d), tok(attn.shape[2]), tok(yg.shape[2]),
                  pl.BlockSpec((None, 8, d), lambda b, i: (b, 0, 0)),
                  _const_spec(gpre.shape), _const_spec(wga.shape), _const_spec(wgb.shape),
                  _const_spec(woa.shape), _const_spec(wob.shape), _const_spec(wout.shape),
                  _const_spec(gpost.shape)],
        out_specs=tok(d),
        out_shape=jax.ShapeDtypeStruct((bsz, seq, d), F32),
        scratch_shapes=[pltpu.VMEM((tm, d), BF16)],
        compiler_params=pltpu.CompilerParams(dimension_semantics=("parallel", "parallel"),
                                             vmem_limit_bytes=VMEM_LIMIT_BYTES),
        name="merge",
    )(x, attn, yg, mod8, gpre, wga, wgb, woa, wob, wout, gpost)


def _mlp_kernel(x_ref, mod_ref, gpre_ref, w1_ref, w2_ref, gpost_ref, o_ref, acc_ref, *, cw):
    x = x_ref[...]
    shift2 = mod_ref[3:4, :]
    scale2 = mod_ref[4:5, :]
    gate2 = mod_ref[5:6, :]
    hb = (_rms(x, gpre_ref[...]) * (1.0 + scale2) + shift2).astype(BF16)
    dff = w1_ref.shape[1]
    for ci, c0 in enumerate(range(0, dff, cw)):
        a = jnp.maximum(_dot(hb, w1_ref[:, c0:c0 + cw]), 0.0)
        part = _dot((a * a).astype(BF16), w2_ref[c0:c0 + cw, :])
        if ci == 0:
            acc_ref[...] = part
        else:
            acc_ref[...] += part
    o_ref[...] = x + gate2 * _rms(acc_ref[...], gpost_ref[...])


def _mlp(x, mod8, gpre, w1, w2, gpost, *, tm):
    bsz, seq, d = x.shape
    tok = lambda width: pl.BlockSpec((None, tm, width), lambda b, i: (b, i, 0))
    return pl.pallas_call(
        functools.partial(_mlp_kernel, cw=1024),
        grid=(bsz, seq // tm),
        in_specs=[tok(d), pl.BlockSpec((None, 8, d), lambda b, i: (b, 0, 0)),
                  _const_spec(gpre.shape), _const_spec(w1.shape), _const_spec(w2.shape),
                  _const_spec(gpost.shape)],
        out_specs=tok(d),
        out_shape=jax.ShapeDtypeStruct((bsz, seq, d), F32),
        scratch_shapes=[pltpu.VMEM((tm, d), F32)],
        compiler_params=pltpu.CompilerParams(dimension_semantics=("parallel", "parallel"),
                                             vmem_limit_bytes=VMEM_LIMIT_BYTES),
        name="mlp",
    )(x, mod8, gpre, w1, w2, gpost)


def _pad_row(v, width, offset=0):
    out = jnp.zeros((1, width), F32)
    return out.at[0, offset:offset + v.shape[0]].set(v.astype(F32))


def kernel(x, c, positions, w_ada, b_ada, g_pre_mix, g_post_mix, w_in, g_q_lat, g_kv_lat, w_uq, w_ukv,
           w_o_attn, conv_w, conv_b, dt_bias, a_log, d_skip, g_ssm_out, w_o_ssm, w_out, g_pre_mlp,
           g_post_mlp, w_ff1, w_ff2):
    bsz, seq, d = x.shape
    depth = w_ada.shape[0]
    tm = min(512, seq)
    tq = min(1024, seq)
    half = D_ROPE // 2
    inv = ROPE_THETA ** (-jnp.arange(half, dtype=F32) / half)
    inv128 = jnp.broadcast_to(inv[:, None], (half, LANES))
    pos3 = positions.reshape(bsz, 1, seq)
    row = lambda v: v.reshape(1, -1).astype(F32)

    for l in range(depth):
        wi = w_in[l]
        o_q, o_kv, o_kr, o_z = 0, R_Q, R_Q + R_KV, R_Q + R_KV + D_ROPE
        o_xbc = o_z + SSM_INNER
        o_dt = o_xbc + SSM_INNER + 2 * SSM_GROUPS * SSM_N
        o_ga = o_dt + SSM_HEADS
        o_gb = o_ga + d
        kr1 = wi[:, o_kr:o_kr + half]
        kr2 = wi[:, o_kr + half:o_kr + D_ROPE]
        wdt = wi[:, o_dt:o_dt + SSM_HEADS]
        wlat = jnp.concatenate(
            [wi[:, o_q:o_kr], kr1, kr2, kr2, kr1, wdt, wdt, jnp.zeros((d, 64), F32)], axis=1).astype(BF16)
        wz = wi[:, o_z:o_xbc].astype(BF16)
        wxbc = wi[:, o_xbc:o_dt].astype(BF16)
        wga = wi[:, o_ga:o_gb].astype(BF16)
        wgb = wi[:, o_gb:o_gb + d].astype(BF16)
        wq3 = w_uq[l].reshape(R_Q, N_HEADS, D_QK)
        q1 = wq3[:, :, D_NOPE:D_NOPE + half]
        q2 = wq3[:, :, D_NOPE + half:]
        wq = jnp.concatenate([wq3[:, :, :D_NOPE], q1, q2, q2, q1], axis=2).transpose(1, 0, 2).astype(BF16)
        wkv = w_ukv[l].reshape(R_KV, N_HEADS, D_NOPE + D_V).transpose(1, 0, 2).astype(BF16)
        dtb = jnp.concatenate([row(dt_bias[l]), row(dt_bias[l]), jnp.zeros((1, 64), F32)], axis=1)
        alog = _pad_row(a_log[l], LANES, 32)
        dsk = jnp.repeat(d_skip[l].astype(F32), SSM_P).reshape(1, SSM_INNER)

        mod = _ada(c, w_ada[l], b_ada[l])
        mod8 = jnp.concatenate([mod.reshape(bsz, 6, d), jnp.zeros((bsz, 2, d), F32)], axis=1)

        q, k, v, z, xs, bm, cm, dta = _front(
            x, pos3, mod8, row(g_pre_mix[l]), wlat, row(g_q_lat[l]), row(g_kv_lat[l]), wq, wkv, wz, wxbc,
            conv_w[l].astype(F32), row(conv_b[l]), dtb, alog, inv128, tm=tm)
        attn = _attention(q, k, v, tq=tq)
        yg = _ssd(xs, bm, cm, dta, z, dsk, row(g_ssm_out[l]))
        x = _merge(x, attn, yg, mod8, row(g_pre_mix[l]), wga, wgb, w_o_attn[l].astype(BF16),
                   w_o_ssm[l].astype(BF16), w_out[l].astype(BF16), row(g_post_mix[l]), tm=tm)
        x = _mlp(x, mod8, row(g_pre_mlp[l]), w_ff1[l].astype(BF16), w_ff2[l].astype(BF16),
                 row(g_post_mlp[l]), tm=tm)
    return x
```

```python
import functools
import math

import jax
import jax.numpy as jnp
from jax import lax
from jax.experimental import pallas as pl
from jax.experimental.pallas import tpu as pltpu

F32 = jnp.float32
BF16 = jnp.bfloat16
HIGHEST = lax.Precision.HIGHEST

N_HEADS = 8
D_NOPE = 128
D_ROPE = 64
D_V = 128
D_QK = D_NOPE + D_ROPE
R_Q = 256
R_KV = 256
ROPE_THETA = 10000.0
SSM_INNER = 2048
SSM_P = 64
SSM_HEADS = 32
SSM_GROUPS = 8
SSM_N = 128
CONV_K = 4
CHUNK = 128
EPS = 1e-6
LOG2E = 1.4426950408889634

VMEM_LIMIT_BYTES = 56 * 1024 * 1024
LANES = 128

_NT = (((1,), (1,)), ((), ()))


def _dot(a, b):
    return jnp.dot(a, b, preferred_element_type=F32)


def _dot_nt(a, b, precision=None):
    return lax.dot_general(a, b, _NT, precision=precision, preferred_element_type=F32)


def _rms(x, g):
    ms = jnp.mean(x * x, axis=-1, keepdims=True)
    return x * lax.rsqrt(ms + EPS) * g


def _sigmoid(x):
    return 1.0 / (1.0 + jnp.exp(-x))


def _const_spec(shape):
    n = len(shape)
    return pl.BlockSpec(shape, lambda *_: (0,) * n, pipeline_mode=pl.Buffered(1))


def _ada_kernel(c_ref, w_ref, b_ref, o_ref):
    c = c_ref[...]
    sc = c * _sigmoid(c)
    o_ref[...] = jnp.dot(sc, w_ref[...], precision=HIGHEST, preferred_element_type=F32) + b_ref[...]


def _ada(c, w_ada, b_ada):
    bsz, d = c.shape
    n = w_ada.shape[1]
    blk = 1024
    return pl.pallas_call(
        _ada_kernel,
        grid=(n // blk,),
        in_specs=[pl.BlockSpec((bsz, d), lambda j: (0, 0)),
                  pl.BlockSpec((d, blk), lambda j: (0, j)),
                  pl.BlockSpec((1, blk), lambda j: (0, j))],
        out_specs=pl.BlockSpec((bsz, blk), lambda j: (0, j)),
        out_shape=jax.ShapeDtypeStruct((bsz, n), F32),
        name="ada",
    )(c, w_ada, b_ada.reshape(1, n))


def _front_kernel(x_ref, pos_ref, mod_ref, gpre_ref, wlat_ref, gq_ref, gkv_ref, wq_ref, wkv_ref,
                  wz_ref, wxbc_ref, convw_ref, convb_ref, dtb_ref, alog_ref, inv_ref,
                  q_ref, k_ref, v_ref, z_ref, xs_ref, bm_ref, cm_ref, dta_ref,
                  carry_ref, cbuf_ref, tin_ref, tout_ref, *, tm, cw, qscale, pitch):
    i = pl.program_id(1)

    @pl.when(i == 0)
    def _():
        carry_ref[...] = jnp.zeros_like(carry_ref)

    x = x_ref[...]
    shift1 = mod_ref[0:1, :]
    scale1 = mod_ref[1:2, :]
    h = _rms(x, gpre_ref[...]) * (1.0 + scale1) + shift1
    hb = h.astype(BF16)

    lat = _dot(hb, wlat_ref[...])
    qn = _rms(lat[:, 0:R_Q], gq_ref[...]).astype(BF16)
    kvn = _rms(lat[:, R_Q:R_Q + R_KV], gkv_ref[...]).astype(BF16)
    kr = lat[:, 512:640]
    dtr = lat[:, 640:768]

    lane = lax.broadcasted_iota(jnp.int32, (1, LANES), 1)
    pre = dtr + dtb_ref[...]
    sp = jnp.maximum(pre, 0.0) + jnp.log1p(jnp.exp(-jnp.abs(pre)))
    arow = jnp.where(lane < 32, 1.0, jnp.where(lane < 64, -jnp.exp(alog_ref[...]), 0.0))
    dta_ref[...] = sp * arow

    ang = jnp.tile(inv_ref[...], (1, tm // LANES)) * pos_ref[...].astype(F32)
    cs = jnp.cos(ang)
    sn = jnp.sin(ang)
    rot = jnp.concatenate([cs, cs, -sn, sn], axis=0).T

    def rope(t):
        u = t * rot
        return u + pltpu.roll(u, 64, axis=1)

    kro = rope(kr)[:, 0:D_ROPE].astype(BF16)

    for hd in range(N_HEADS):
        qh = _dot(qn, wq_ref[hd])
        q_ref[hd, :, 0:D_NOPE] = (qh[:, 0:D_NOPE] * qscale).astype(BF16)
        q_ref[hd, :, D_NOPE:D_QK] = (rope(qh[:, 128:256])[:, 0:D_ROPE] * qscale).astype(BF16)
        kvh = _dot(kvn, wkv_ref[hd])
        k_ref[hd, :, 0:D_NOPE] = kvh[:, 0:D_NOPE].astype(BF16)
        k_ref[hd, :, D_NOPE:D_QK] = kro
        v_ref[hd] = kvh[:, 128:256].astype(BF16)

    d = x.shape[1]
    rows_p = 8 * pitch
    for j in range(d // LANES):
        tin_ref[j, 0:tm, :] = h[:, j * LANES:(j + 1) * LANES]
        tin_ref[j, tm:rows_p, :] = jnp.zeros((rows_p - tm, LANES), F32)
    hp = jnp.concatenate(
        [jnp.concatenate([tin_ref[j, pl.ds(a, 8, stride=pitch), :] for j in range(d // LANES)], axis=1)
         for a in range(pitch)], axis=0).astype(BF16)
    sub = lax.broadcasted_iota(jnp.int32, (8, cw), 0)
    n_xbc = wxbc_ref.shape[1]
    z_every = n_xbc // SSM_INNER
    for ci, c0 in enumerate(range(0, n_xbc, cw)):
        slot = ci % 2
        cols = slice(c0, c0 + cw)
        u = _dot(hp, wxbc_ref[:, cols])
        if ci % z_every == 0:
            zc = slice((ci // z_every) * cw, (ci // z_every + 1) * cw)
            z_ref[:, zc] = _dot(hb, wz_ref[:, zc]).astype(BF16)
        for j in range(1, CONV_K):
            grp = pltpu.roll(u[(pitch - j) * 8:(pitch - j + 1) * 8, :], 1, axis=0)
            prev = carry_ref[CONV_K - 1 - j:CONV_K - j, cols]
            cbuf_ref[slot, (CONV_K - 1 - j) * 8:(CONV_K - j) * 8, :] = jnp.where(sub == 0, prev, grp)
        hdr = (CONV_K - 1) * 8
        cbuf_ref[slot, hdr:hdr + rows_p, :] = u
        for j in range(1, CONV_K):
            t = tm - j
            r = (t % pitch) * 8 + t // pitch
            carry_ref[CONV_K - 1 - j:CONV_K - j, cols] = u[r:r + 1, :]
        acc = convb_ref[:, cols]
        for kk in range(CONV_K):
            acc = acc + cbuf_ref[slot, kk * 8:kk * 8 + rows_p, :] * convw_ref[kk:kk + 1, cols]
        for a in range(pitch):
            for jj in range(cw // LANES):
                tout_ref[slot, jj, pl.ds(a, 8, stride=pitch), :] = acc[a * 8:(a + 1) * 8,
                                                                      jj * LANES:(jj + 1) * LANES]
        v = jnp.concatenate([tout_ref[slot, jj, 0:tm, :] for jj in range(cw // LANES)], axis=1)
        sv = (v * _sigmoid(v)).astype(BF16)
        if c0 < SSM_INNER:
            xs_ref[:, cols] = sv
        elif c0 < SSM_INNER + SSM_GROUPS * SSM_N:
            o0 = c0 - SSM_INNER
            bm_ref[:, o0:o0 + cw] = sv
        else:
            o0 = c0 - SSM_INNER - SSM_GROUPS * SSM_N
            cm_ref[:, o0:o0 + cw] = sv


def _front(x, pos3, mod8, gpre, wlat, gq, gkv, wq, wkv, wz, wxbc, convw, convb, dtb, alog, inv128, *, tm):
    bsz, seq, d = x.shape
    cw = 256
    assert tm % 64 == 0
    pitch = tm // 8 + 4
    qscale = float(D_QK ** -0.5 * LOG2E)
    n_bc = SSM_GROUPS * SSM_N
    tok = lambda width: pl.BlockSpec((None, tm, width), lambda b, i: (b, i, 0))
    head = lambda width: pl.BlockSpec((None, N_HEADS, tm, width), lambda b, i: (b, 0, i, 0))
    in_specs = [
        tok(d),
        pl.BlockSpec((None, 1, tm), lambda b, i: (b, 0, i)),
        pl.BlockSpec((None, 8, d), lambda b, i: (b, 0, 0)),
        _const_spec(gpre.shape), _const_spec(wlat.shape), _const_spec(gq.shape), _const_spec(gkv.shape),
        _const_spec(wq.shape), _const_spec(wkv.shape), _const_spec(wz.shape), _const_spec(wxbc.shape),
        _const_spec(convw.shape), _const_spec(convb.shape), _const_spec(dtb.shape), _const_spec(alog.shape),
        _const_spec(inv128.shape),
    ]
    out_specs = [head(D_QK), head(D_QK), head(D_V), tok(SSM_INNER), tok(SSM_INNER), tok(n_bc), tok(n_bc),
                 tok(LANES)]
    out_shape = [
        jax.ShapeDtypeStruct((bsz, N_HEADS, seq, D_QK), BF16),
        jax.ShapeDtypeStruct((bsz, N_HEADS, seq, D_QK), BF16),
        jax.ShapeDtypeStruct((bsz, N_HEADS, seq, D_V), BF16),
        jax.ShapeDtypeStruct((bsz, seq, SSM_INNER), BF16),
        jax.ShapeDtypeStruct((bsz, seq, SSM_INNER), BF16),
        jax.ShapeDtypeStruct((bsz, seq, n_bc), BF16),
        jax.ShapeDtypeStruct((bsz, seq, n_bc), BF16),
        jax.ShapeDtypeStruct((bsz, seq, LANES), F32),
    ]
    return pl.pallas_call(
        functools.partial(_front_kernel, tm=tm, cw=cw, qscale=qscale, pitch=pitch),
        grid=(bsz, seq // tm),
        in_specs=in_specs,
        out_specs=out_specs,
        out_shape=out_shape,
        scratch_shapes=[pltpu.VMEM((8, wxbc.shape[1]), F32),
                        pltpu.VMEM((2, (CONV_K - 1) * 8 + 8 * pitch, cw), F32),
                        pltpu.VMEM((d // LANES, 8 * pitch, LANES), F32),
                        pltpu.VMEM((2, cw // LANES, 8 * pitch, LANES), F32)],
        compiler_params=pltpu.CompilerParams(dimension_semantics=("parallel", "arbitrary"),
                                             vmem_limit_bytes=VMEM_LIMIT_BYTES),
        name="front",
    )(x, pos3, mod8, gpre, wlat, gq, gkv, wq, wkv, wz, wxbc, convw, convb, dtb, alog, inv128)


def _attn_kernel(q_ref, k_ref, v_ref, o_ref, sa_ref, sb_ref, m_ref, l_ref, acc_ref, *, tq, tk, n_tiles):
    n_rep = tk // LANES
    bufs = (sa_ref, sb_ref)
    row = lax.broadcasted_iota(jnp.int32, (tq, tk), 0)
    col = lax.broadcasted_iota(jnp.int32, (tq, tk), 1)
    row_b = lax.broadcasted_iota(jnp.int32, (tk, tk), 0)
    col_b = lax.broadcasted_iota(jnp.int32, (tk, tk), 1)

    def scores(r0, n_rows, blk):
        return _dot_nt(q_ref[r0:r0 + n_rows, :], k_ref[blk * tk:(blk + 1) * tk, :])

    def update(par, rows, blk, s):
        m_prev = m_ref[par, rows, :]
        m_new = jnp.maximum(m_prev, jnp.max(s, axis=-1, keepdims=True))
        alpha = jnp.exp2(m_prev - m_new)
        p = jnp.exp2(s - jnp.tile(m_new, (1, n_rep)))
        m_ref[par, rows, :] = m_new
        l_ref[par, rows, :] = alpha * l_ref[par, rows, :] + jnp.sum(p, axis=-1, keepdims=True)
        pv = _dot(p.astype(BF16), v_ref[blk * tk:(blk + 1) * tk, :])
        return alpha, pv

    allrows = slice(0, tq)
    bot = slice(tk, tq)
    cur = 0
    bufs[cur][...] = scores(0, tq, 0)
    for i in range(n_tiles):
        par = i % 2
        r0 = i * tq
        m_ref[par] = jnp.full((tq, LANES), -jnp.inf, F32)
        l_ref[par] = jnp.zeros((tq, LANES), F32)
        acc_ref[par] = jnp.zeros((tq, D_V), F32)
        for blk in range(2 * i):
            alpha, pv = update(par, allrows, blk, bufs[cur][...])
            bufs[1 - cur][...] = scores(r0, tq, blk + 1)
            acc_ref[par] = alpha * acc_ref[par] + pv
            cur = 1 - cur
        alpha, pv = update(par, allrows, 2 * i, jnp.where(col <= row, bufs[cur][...], -jnp.inf))
        bufs[1 - cur][0:tk, :] = scores(r0 + tk, tk, 2 * i + 1)
        acc_ref[par] = alpha * acc_ref[par] + pv
        alpha, pv = update(par, bot, 2 * i + 1, jnp.where(col_b <= row_b, bufs[1 - cur][0:tk, :], -jnp.inf))
        if i + 1 < n_tiles:
            bufs[cur][...] = scores(r0 + tq, tq, 0)
        acc_ref[par, bot, :] = alpha * acc_ref[par, bot, :] + pv
        o_ref[r0:r0 + tq, :] = (acc_ref[par] / l_ref[par]).astype(BF16)


def _attention(q, k, v, *, tq):
    bsz, nh, seq, _ = q.shape
    whole = lambda width: pl.BlockSpec((None, None, seq, width), lambda b, h: (b, h, 0, 0))
    return pl.pallas_call(
        functools.partial(_attn_kernel, tq=tq, tk=tq // 2, n_tiles=seq // tq),
        grid=(bsz, nh),
        in_specs=[whole(D_QK), whole(D_QK), whole(D_V)],
        out_specs=pl.BlockSpec((None, seq, D_V), lambda b, h: (b, 0, h)),
        out_shape=jax.ShapeDtypeStruct((bsz, seq, nh * D_V), BF16),
        scratch_shapes=[pltpu.VMEM((tq, tq // 2), F32), pltpu.VMEM((tq, tq // 2), F32),
                        pltpu.VMEM((2, tq, LANES), F32), pltpu.VMEM((2, tq, LANES), F32),
                        pltpu.VMEM((2, tq, D_V), F32)],
        compiler_params=pltpu.CompilerParams(dimension_semantics=("parallel", "parallel"),
                                             vmem_limit_bytes=VMEM_LIMIT_BYTES),
        name="attn",
    )(q, k, v)


def _ssd_kernel(xs_ref, bm_ref, cm_ref, dta_ref, z_ref, dsk_ref, gss_ref, o_ref,
                state_ref, rows_ref, *, n_sub):
    ci = pl.program_id(1)

    @pl.when(ci == 0)
    def _():
        state_ref[...] = jnp.zeros_like(state_ref)

    L = CHUNK
    r_i = lax.broadcasted_iota(jnp.int32, (L, L), 0)
    c_i = lax.broadcasted_iota(jnp.int32, (L, L), 1)
    tril = c_i <= r_i
    tril_f = tril.astype(F32)
    eye_b = (r_i == c_i).astype(BF16)
    lane = lax.broadcasted_iota(jnp.int32, (1, L), 1)
    in_a = (lane >= 32) & (lane < 64)
    first = lane < 64

    for sc in range(n_sub):
        rows = slice(sc * L, (sc + 1) * L)
        dta = dta_ref[rows, :]
        cum = jnp.dot(tril_f, dta, precision=HIGHEST, preferred_element_type=F32)
        acm2 = jnp.where(in_a, cum * LOG2E, 0.0)
        comb_t = jnp.where(in_a, acm2, dta).T
        dt_t = comb_t[0:SSM_HEADS]
        a2_t = comb_t[SSM_HEADS:2 * SSM_HEADS]
        rows_ref[sc, 0] = a2_t - jnp.log2(dt_t)
        rows_ref[sc, 1] = jnp.exp2(a2_t[:, L - 1:L] - a2_t) * dt_t
        cd_all = jnp.exp2(acm2[L - 1:L, :])

        for g in range(SSM_GROUPS):
            cg = cm_ref[rows, g * SSM_N:(g + 1) * SSM_N]
            bg = bm_ref[rows, g * SSM_N:(g + 1) * SSM_N]
            cb = _dot_nt(cg, bg)
            bt = _dot_nt(eye_b, bg)
            cg32 = cg.astype(F32)
            ys = []
            for pr in range(2):
                hp = g * 2 + pr
                xpair = xs_ref[rows, hp * 128:(hp + 1) * 128]
                s_old = state_ref[hp]
                rhs = jnp.concatenate([xpair, s_old.astype(BF16)], axis=0)
                lhs_parts = []
                bw_parts = []
                for e in range(2):
                    hh = hp * 2 + e
                    colb = jnp.broadcast_to(acm2[:, 32 + hh:33 + hh], (L, L))
                    dec = jnp.exp2(jnp.where(tril, colb - rows_ref[sc, 0, hh:hh + 1, :], -jnp.inf))
                    mh = (cb * dec).astype(BF16)
                    ce = (cg32 * jnp.exp2(colb)).astype(BF16)
                    lhs_parts.append(jnp.concatenate([mh, ce], axis=1))
                    bw_parts.append((bt * rows_ref[sc, 1, hh:hh + 1, :]).astype(BF16))
                yy = _dot(jnp.concatenate(lhs_parts, axis=0), rhs)
                ss = _dot(jnp.concatenate(bw_parts, axis=0), xpair)
                ys.append(jnp.where(first, yy[0:L], yy[L:2 * L]))
                h0 = hp * 2
                cd0 = cd_all[:, 32 + h0:33 + h0]
                cd1 = cd_all[:, 33 + h0:34 + h0]
                cdrow = jnp.where(first, cd0, cd1)
                state_ref[hp] = cdrow * s_old + jnp.where(first, ss[0:SSM_N], ss[SSM_N:2 * SSM_N])
            sl = slice(g * 256, (g + 1) * 256)
            xg = xs_ref[rows, sl].astype(F32)
            y = jnp.concatenate(ys, axis=1) + dsk_ref[:, sl] * xg
            zg = z_ref[rows, sl].astype(F32)
            yz = y * (zg * _sigmoid(zg))
            o_ref[rows, sl] = _rms(yz, gss_ref[:, sl]).astype(BF16)


def _ssd(xs, bm, cm, dta, z, dsk, gss, *, n_sub):
    bsz, seq, _ = xs.shape
    n_bc = SSM_GROUPS * SSM_N
    rows = n_sub * CHUNK
    tok = lambda width: pl.BlockSpec((None, rows, width), lambda b, c: (b, c, 0))
    return pl.pallas_call(
        functools.partial(_ssd_kernel, n_sub=n_sub),
        grid=(bsz, seq // rows),
        in_specs=[tok(SSM_INNER), tok(n_bc), tok(n_bc), tok(LANES), tok(SSM_INNER),
                  _const_spec(dsk.shape), _const_spec(gss.shape)],
        out_specs=tok(SSM_INNER),
        out_shape=jax.ShapeDtypeStruct((bsz, seq, SSM_INNER), BF16),
        scratch_shapes=[pltpu.VMEM((SSM_HEADS // 2, SSM_N, 128), F32),
                        pltpu.VMEM((n_sub, 2, SSM_HEADS, CHUNK), F32)],
        compiler_params=pltpu.CompilerParams(dimension_semantics=("parallel", "arbitrary"),
                                             vmem_limit_bytes=VMEM_LIMIT_BYTES),
        name="ssd",
    )(xs, bm, cm, dta, z, dsk, gss)


def _merge_kernel(x_ref, attn_ref, yg_ref, mod_ref, gpre_ref, wga_ref, wgb_ref, woa_ref, wob_ref,
                  wout_ref, gpost_ref, o_ref, mb_ref, *, cw):
    x = x_ref[...]
    shift1 = mod_ref[0:1, :]
    scale1 = mod_ref[1:2, :]
    gate1 = mod_ref[2:3, :]
    hb = (_rms(x, gpre_ref[...]) * (1.0 + scale1) + shift1).astype(BF16)
    at = attn_ref[...]
    yg = yg_ref[...]
    d = x.shape[1]
    for c0 in range(0, d, cw):
        sl = slice(c0, c0 + cw)
        ga = _sigmoid(_dot(hb, wga_ref[:, sl]))
        gb = _sigmoid(_dot(hb, wgb_ref[:, sl]))
        merged = ga * _dot(at, woa_ref[:, sl]) + gb * _dot(yg, wob_ref[:, sl])
        mb_ref[:, sl] = merged.astype(BF16)
    mix = _dot(mb_ref[...], wout_ref[...])
    o_ref[...] = x + gate1 * _rms(mix, gpost_ref[...])


def _merge(x, attn, yg, mod8, gpre, wga, wgb, woa, wob, wout, gpost, *, tm):
    bsz, seq, d = x.shape
    tok = lambda width: pl.BlockSpec((None, tm, width), lambda b, i: (b, i, 0))
    return pl.pallas_call(
        functools.partial(_merge_kernel, cw=512),
        grid=(bsz, seq // tm),
        in_specs=[tok(d), tok(attn.shape[2]), tok(yg.shape[2]),
                  pl.BlockSpec((None, 8, d), lambda b, i: (b, 0, 0)),
                  _const_spec(gpre.shape), _const_spec(wga.shape), _const_spec(wgb.shape),
                  _const_spec(woa.shape), _const_spec(wob.shape), _const_spec(wout.shape),
                  _const_spec(gpost.shape)],
        out_specs=tok(d),
        out_shape=jax.ShapeDtypeStruct((bsz, seq, d), F32),
        scratch_shapes=[pltpu.VMEM((tm, d), BF16)],
        compiler_params=pltpu.CompilerParams(dimension_semantics=("parallel", "parallel"),
                                             vmem_limit_bytes=VMEM_LIMIT_BYTES),
        name="merge",
    )(x, attn, yg, mod8, gpre, wga, wgb, woa, wob, wout, gpost)


def _mlp_kernel(x_ref, mod_ref, gpre_ref, w1_ref, w2_ref, gpost_ref, o_ref, acc_ref, *, cw):
    x = x_ref[...]
    shift2 = mod_ref[3:4, :]
    scale2 = mod_ref[4:5, :]
    gate2 = mod_ref[5:6, :]
    hb = (_rms(x, gpre_ref[...]) * (1.0 + scale2) + shift2).astype(BF16)
    dff = w1_ref.shape[1]
    for ci, c0 in enumerate(range(0, dff, cw)):
        a = jnp.maximum(_dot(hb, w1_ref[:, c0:c0 + cw]), 0.0)
        part = _dot((a * a).astype(BF16), w2_ref[c0:c0 + cw, :])
        if ci == 0:
            acc_ref[...] = part
        else:
            acc_ref[...] += part
    o_ref[...] = x + gate2 * _rms(acc_ref[...], gpost_ref[...])


def _mlp(x, mod8, gpre, w1, w2, gpost, *, tm):
    bsz, seq, d = x.shape
    tok = lambda width: pl.BlockSpec((None, tm, width), lambda b, i: (b, i, 0))
    return pl.pallas_call(
        functools.partial(_mlp_kernel, cw=1024),
        grid=(bsz, seq // tm),
        in_specs=[tok(d), pl.BlockSpec((None, 8, d), lambda b, i: (b, 0, 0)),
                  _const_spec(gpre.shape), _const_spec(w1.shape), _const_spec(w2.shape),
                  _const_spec(gpost.shape)],
        out_specs=tok(d),
        out_shape=jax.ShapeDtypeStruct((bsz, seq, d), F32),
        scratch_shapes=[pltpu.VMEM((tm, d), F32)],
        compiler_params=pltpu.CompilerParams(dimension_semantics=("parallel", "parallel"),
                                             vmem_limit_bytes=VMEM_LIMIT_BYTES),
        name="mlp",
    )(x, mod8, gpre, w1, w2, gpost)


def _pad_row(v, width, offset=0):
    out = jnp.zeros((1, width), F32)
    return out.at[0, offset:offset + v.shape[0]].set(v.astype(F32))


def kernel(x, c, positions, w_ada, b_ada, g_pre_mix, g_post_mix, w_in, g_q_lat, g_kv_lat, w_uq, w_ukv,
           w_o_attn, conv_w, conv_b, dt_bias, a_log, d_skip, g_ssm_out, w_o_ssm, w_out, g_pre_mlp,
           g_post_mlp, w_ff1, w_ff2):
    bsz, seq, d = x.shape
    depth = w_ada.shape[0]
    tm = min(512, seq)
    tq = min(1024, seq)
    half = D_ROPE // 2
    inv = ROPE_THETA ** (-jnp.arange(half, dtype=F32) / half)
    inv128 = jnp.broadcast_to(inv[:, None], (half, LANES))
    pos3 = positions.reshape(bsz, 1, seq)
    row = lambda v: v.reshape(1, -1).astype(F32)

    for l in range(depth):
        wi = w_in[l]
        o_q, o_kv, o_kr, o_z = 0, R_Q, R_Q + R_KV, R_Q + R_KV + D_ROPE
        o_xbc = o_z + SSM_INNER
        o_dt = o_xbc + SSM_INNER + 2 * SSM_GROUPS * SSM_N
        o_ga = o_dt + SSM_HEADS
        o_gb = o_ga + d
        kr1 = wi[:, o_kr:o_kr + half]
        kr2 = wi[:, o_kr + half:o_kr + D_ROPE]
        wdt = wi[:, o_dt:o_dt + SSM_HEADS]
        wlat = jnp.concatenate(
            [wi[:, o_q:o_kr], kr1, kr2, kr2, kr1, wdt, wdt, jnp.zeros((d, 64), F32)], axis=1).astype(BF16)
        wz = wi[:, o_z:o_xbc].astype(BF16)
        wxbc = wi[:, o_xbc:o_dt].astype(BF16)
        wga = wi[:, o_ga:o_gb].astype(BF16)
        wgb = wi[:, o_gb:o_gb + d].astype(BF16)
        wq3 = w_uq[l].reshape(R_Q, N_HEADS, D_QK)
        q1 = wq3[:, :, D_NOPE:D_NOPE + half]
        q2 = wq3[:, :, D_NOPE + half:]
        wq = jnp.concatenate([wq3[:, :, :D_NOPE], q1, q2, q2, q1], axis=2).transpose(1, 0, 2).astype(BF16)
        wkv = w_ukv[l].reshape(R_KV, N_HEADS, D_NOPE + D_V).transpose(1, 0, 2).astype(BF16)
        dtb = jnp.concatenate([row(dt_bias[l]), row(dt_bias[l]), jnp.zeros((1, 64), F32)], axis=1)
        alog = _pad_row(a_log[l], LANES, 32)
        dsk = jnp.repeat(d_skip[l].astype(F32), SSM_P).reshape(1, SSM_INNER)

        mod = _ada(c, w_ada[l], b_ada[l])
        mod8 = jnp.concatenate([mod.reshape(bsz, 6, d), jnp.zeros((bsz, 2, d), F32)], axis=1)

        q, k, v, z, xs, bm, cm, dta = _front(
            x, pos3, mod8, row(g_pre_mix[l]), wlat, row(g_q_lat[l]), row(g_kv_lat[l]), wq, wkv, wz, wxbc,
            conv_w[l].astype(F32), row(conv_b[l]), dtb, alog, inv128, tm=tm)
        attn = _attention(q, k, v, tq=tq)
        yg = _ssd(xs, bm, cm, dta, z, dsk, row(g_ssm_out[l]), n_sub=min(4, seq // CHUNK))
        x = _merge(x, attn, yg, mod8, row(g_pre_mix[l]), wga, wgb, w_o_attn[l].astype(BF16),
                   w_o_ssm[l].astype(BF16), w_out[l].astype(BF16), row(g_post_mix[l]), tm=tm)
        x = _mlp(x, mod8, row(g_pre_mlp[l]), w_ff1[l].astype(BF16), w_ff2[l].astype(BF16),
                 row(g_post_mlp[l]), tm=tm)
    return x
```

```python
import functools

import jax
import jax.numpy as jnp
from jax import lax
from jax.experimental import pallas as pl
from jax.experimental.pallas import tpu as pltpu

F32 = jnp.float32
BF16 = jnp.bfloat16
HIGHEST = lax.Precision.HIGHEST

N_HEADS = 8
D_NOPE = 128
D_ROPE = 64
D_V = 128
D_QK = D_NOPE + D_ROPE
R_Q = 256
R_KV = 256
ROPE_THETA = 10000.0
SSM_INNER = 2048
SSM_P = 64
SSM_HEADS = 32
SSM_GROUPS = 8
SSM_N = 128
CONV_K = 4
CHUNK = 128
EPS = 1e-6
LOG2E = 1.4426950408889634

VMEM_LIMIT_BYTES = 56 * 1024 * 1024
LANES = 128

_NT = (((1,), (1,)), ((), ()))


def _dot(a, b):
    return jnp.dot(a, b, preferred_element_type=F32)


def _dot_nt(a, b, precision=None):
    return lax.dot_general(a, b, _NT, precision=precision, preferred_element_type=F32)


def _rms(x, g):
    ms = jnp.mean(x * x, axis=-1, keepdims=True)
    return x * lax.rsqrt(ms + EPS) * g


def _sigmoid(x):
    return 1.0 / (1.0 + jnp.exp(-x))


def _const_spec(shape):
    n = len(shape)
    return pl.BlockSpec(shape, lambda *_: (0,) * n, pipeline_mode=pl.Buffered(1))


def _ada_kernel(c_ref, w_ref, b_ref, o_ref):
    c = c_ref[...]
    sc = c * _sigmoid(c)
    o_ref[...] = jnp.dot(sc, w_ref[...], precision=HIGHEST, preferred_element_type=F32) + b_ref[...]


def _ada(c, w_ada, b_ada):
    bsz, d = c.shape
    n = w_ada.shape[1]
    blk = 1024
    return pl.pallas_call(
        _ada_kernel,
        grid=(n // blk,),
        in_specs=[pl.BlockSpec((bsz, d), lambda j: (0, 0)),
                  pl.BlockSpec((d, blk), lambda j: (0, j)),
                  pl.BlockSpec((1, blk), lambda j: (0, j))],
        out_specs=pl.BlockSpec((bsz, blk), lambda j: (0, j)),
        out_shape=jax.ShapeDtypeStruct((bsz, n), F32),
        name="ada",
    )(c, w_ada, b_ada.reshape(1, n))


def _front_kernel(x_ref, pos_ref, mod_ref, gpre_ref, wlat_ref, gq_ref, gkv_ref, wq_ref, wkv_ref,
                  wz_ref, wxbc_ref, convw_ref, convb_ref, dtb_ref, alog_ref, inv_ref,
                  q_ref, k_ref, v_ref, z_ref, xs_ref, bm_ref, cm_ref, dta_ref,
                  carry_ref, cbuf_ref, tin_ref, tout_ref, *, tm, cw, qscale, pitch):
    i = pl.program_id(1)

    @pl.when(i == 0)
    def _():
        carry_ref[...] = jnp.zeros_like(carry_ref)

    x = x_ref[...]
    shift1 = mod_ref[0:1, :]
    scale1 = mod_ref[1:2, :]
    h = _rms(x, gpre_ref[...]) * (1.0 + scale1) + shift1
    hb = h.astype(BF16)

    lat = _dot(hb, wlat_ref[...])
    qn = _rms(lat[:, 0:R_Q], gq_ref[...]).astype(BF16)
    kvn = _rms(lat[:, R_Q:R_Q + R_KV], gkv_ref[...]).astype(BF16)
    kr = lat[:, 512:640]
    dtr = lat[:, 640:768]

    lane = lax.broadcasted_iota(jnp.int32, (1, LANES), 1)
    pre = dtr + dtb_ref[...]
    sp = jnp.maximum(pre, 0.0) + jnp.log1p(jnp.exp(-jnp.abs(pre)))
    arow = jnp.where(lane < 32, 1.0, jnp.where(lane < 64, -jnp.exp(alog_ref[...]), 0.0))
    dta_ref[...] = sp * arow

    ang = jnp.tile(inv_ref[...], (1, tm // LANES)) * pos_ref[...].astype(F32)
    cs = jnp.cos(ang)
    sn = jnp.sin(ang)
    rot = jnp.concatenate([cs, cs, -sn, sn], axis=0).T

    def rope(t):
        u = t * rot
        return u + pltpu.roll(u, 64, axis=1)

    kro = rope(kr)[:, 0:D_ROPE].astype(BF16)

    for hd in range(N_HEADS):
        qh = _dot(qn, wq_ref[hd])
        q_ref[hd, :, 0:D_NOPE] = (qh[:, 0:D_NOPE] * qscale).astype(BF16)
        q_ref[hd, :, D_NOPE:D_QK] = (rope(qh[:, 128:256])[:, 0:D_ROPE] * qscale).astype(BF16)
        kvh = _dot(kvn, wkv_ref[hd])
        k_ref[hd, :, 0:D_NOPE] = kvh[:, 0:D_NOPE].astype(BF16)
        k_ref[hd, :, D_NOPE:D_QK] = kro
        v_ref[hd] = kvh[:, 128:256].astype(BF16)

    d = x.shape[1]
    rows_p = 8 * pitch
    for j in range(d // LANES):
        tin_ref[j, 0:tm, :] = h[:, j * LANES:(j + 1) * LANES]
        tin_ref[j, tm:rows_p, :] = jnp.zeros((rows_p - tm, LANES), F32)
    hp = jnp.concatenate(
        [jnp.concatenate([tin_ref[j, pl.ds(a, 8, stride=pitch), :] for j in range(d // LANES)], axis=1)
         for a in range(pitch)], axis=0).astype(BF16)
    sub = lax.broadcasted_iota(jnp.int32, (8, cw), 0)
    n_xbc = wxbc_ref.shape[1]
    z_every = n_xbc // SSM_INNER
    for ci, c0 in enumerate(range(0, n_xbc, cw)):
        slot = ci % 2
        cols = slice(c0, c0 + cw)
        u = _dot(hp, wxbc_ref[:, cols])
        if ci % z_every == 0:
            zc = slice((ci // z_every) * cw, (ci // z_every + 1) * cw)
            z_ref[:, zc] = _dot(hb, wz_ref[:, zc]).astype(BF16)
        for j in range(1, CONV_K):
            grp = pltpu.roll(u[(pitch - j) * 8:(pitch - j + 1) * 8, :], 1, axis=0)
            prev = carry_ref[CONV_K - 1 - j:CONV_K - j, cols]
            cbuf_ref[slot, (CONV_K - 1 - j) * 8:(CONV_K - j) * 8, :] = jnp.where(sub == 0, prev, grp)
        hdr = (CONV_K - 1) * 8
        cbuf_ref[slot, hdr:hdr + rows_p, :] = u
        for j in range(1, CONV_K):
            t = tm - j
            r = (t % pitch) * 8 + t // pitch
            carry_ref[CONV_K - 1 - j:CONV_K - j, cols] = u[r:r + 1, :]
        acc = convb_ref[:, cols]
        for kk in range(CONV_K):
            acc = acc + cbuf_ref[slot, kk * 8:kk * 8 + rows_p, :] * convw_ref[kk:kk + 1, cols]
        for a in range(pitch):
            for jj in range(cw // LANES):
                tout_ref[slot, jj, pl.ds(a, 8, stride=pitch), :] = acc[a * 8:(a + 1) * 8,
                                                                      jj * LANES:(jj + 1) * LANES]
        v = jnp.concatenate([tout_ref[slot, jj, 0:tm, :] for jj in range(cw // LANES)], axis=1)
        sv = (v * _sigmoid(v)).astype(BF16)
        if c0 < SSM_INNER:
            xs_ref[:, cols] = sv
        elif c0 < SSM_INNER + SSM_GROUPS * SSM_N:
            o0 = c0 - SSM_INNER
            bm_ref[:, o0:o0 + cw] = sv
        else:
            o0 = c0 - SSM_INNER - SSM_GROUPS * SSM_N
            cm_ref[:, o0:o0 + cw] = sv


def _front(x, pos3, mod8, gpre, wlat, gq, gkv, wq, wkv, wz, wxbc, convw, convb, dtb, alog, inv128, *, tm):
    bsz, seq, d = x.shape
    cw = 256
    assert tm % 64 == 0
    pitch = tm // 8 + 4
    qscale = float(D_QK ** -0.5 * LOG2E)
    n_bc = SSM_GROUPS * SSM_N
    tok = lambda width: pl.BlockSpec((None, tm, width), lambda b, i: (b, i, 0))
    head = lambda width: pl.BlockSpec((None, N_HEADS, tm, width), lambda b, i: (b, 0, i, 0))
    in_specs = [
        tok(d),
        pl.BlockSpec((None, 1, tm), lambda b, i: (b, 0, i)),
        pl.BlockSpec((None, 8, d), lambda b, i: (b, 0, 0)),
        _const_spec(gpre.shape), _const_spec(wlat.shape), _const_spec(gq.shape), _const_spec(gkv.shape),
        _const_spec(wq.shape), _const_spec(wkv.shape), _const_spec(wz.shape), _const_spec(wxbc.shape),
        _const_spec(convw.shape), _const_spec(convb.shape), _const_spec(dtb.shape), _const_spec(alog.shape),
        _const_spec(inv128.shape),
    ]
    out_specs = [head(D_QK), head(D_QK), head(D_V), tok(SSM_INNER), tok(SSM_INNER), tok(n_bc), tok(n_bc),
                 tok(LANES)]
    out_shape = [
        jax.ShapeDtypeStruct((bsz, N_HEADS, seq, D_QK), BF16),
        jax.ShapeDtypeStruct((bsz, N_HEADS, seq, D_QK), BF16),
        jax.ShapeDtypeStruct((bsz, N_HEADS, seq, D_V), BF16),
        jax.ShapeDtypeStruct((bsz, seq, SSM_INNER), BF16),
        jax.ShapeDtypeStruct((bsz, seq, SSM_INNER), BF16),
        jax.ShapeDtypeStruct((bsz, seq, n_bc), BF16),
        jax.ShapeDtypeStruct((bsz, seq, n_bc), BF16),
        jax.ShapeDtypeStruct((bsz, seq, LANES), F32),
    ]
    return pl.pallas_call(
        functools.partial(_front_kernel, tm=tm, cw=cw, qscale=qscale, pitch=pitch),
        grid=(bsz, seq // tm),
        in_specs=in_specs,
        out_specs=out_specs,
        out_shape=out_shape,
        scratch_shapes=[pltpu.VMEM((8, wxbc.shape[1]), F32),
                        pltpu.VMEM((2, (CONV_K - 1) * 8 + 8 * pitch, cw), F32),
                        pltpu.VMEM((d // LANES, 8 * pitch, LANES), F32),
                        pltpu.VMEM((2, cw // LANES, 8 * pitch, LANES), F32)],
        compiler_params=pltpu.CompilerParams(dimension_semantics=("parallel", "arbitrary"),
                                             vmem_limit_bytes=VMEM_LIMIT_BYTES),
        name="front",
    )(x, pos3, mod8, gpre, wlat, gq, gkv, wq, wkv, wz, wxbc, convw, convb, dtb, alog, inv128)


def _attn_kernel(q_ref, k_ref, v_ref, o_ref, sa_ref, sb_ref, m_ref, l_ref, acc_ref, *, tq, tk, n_tiles):
    n_rep = tk // LANES
    bufs = (sa_ref, sb_ref)
    row = lax.broadcasted_iota(jnp.int32, (tq, tk), 0)
    col = lax.broadcasted_iota(jnp.int32, (tq, tk), 1)
    row_b = lax.broadcasted_iota(jnp.int32, (tk, tk), 0)
    col_b = lax.broadcasted_iota(jnp.int32, (tk, tk), 1)

    def scores(r0, n_rows, blk):
        return _dot_nt(q_ref[r0:r0 + n_rows, :], k_ref[blk * tk:(blk + 1) * tk, :])

    def update(par, rows, blk, s):
        m_prev = m_ref[par, rows, :]
        m_new = jnp.maximum(m_prev, jnp.max(s, axis=-1, keepdims=True))
        alpha = jnp.exp2(m_prev - m_new)
        p = jnp.exp2(s - jnp.tile(m_new, (1, n_rep)))
        m_ref[par, rows, :] = m_new
        l_ref[par, rows, :] = alpha * l_ref[par, rows, :] + jnp.sum(p, axis=-1, keepdims=True)
        pv = _dot(p.astype(BF16), v_ref[blk * tk:(blk + 1) * tk, :])
        return alpha, pv

    allrows = slice(0, tq)
    bot = slice(tk, tq)
    cur = 0
    bufs[cur][...] = scores(0, tq, 0)
    for i in range(n_tiles):
        par = i % 2
        r0 = i * tq
        m_ref[par] = jnp.full((tq, LANES), -jnp.inf, F32)
        l_ref[par] = jnp.zeros((tq, LANES), F32)
        acc_ref[par] = jnp.zeros((tq, D_V), F32)
        for blk in range(2 * i):
            alpha, pv = update(par, allrows, blk, bufs[cur][...])
            bufs[1 - cur][...] = scores(r0, tq, blk + 1)
            acc_ref[par] = alpha * acc_ref[par] + pv
            cur = 1 - cur
        alpha, pv = update(par, allrows, 2 * i, jnp.where(col <= row, bufs[cur][...], -jnp.inf))
        bufs[1 - cur][0:tk, :] = scores(r0 + tk, tk, 2 * i + 1)
        acc_ref[par] = alpha * acc_ref[par] + pv
        alpha, pv = update(par, bot, 2 * i + 1, jnp.where(col_b <= row_b, bufs[1 - cur][0:tk, :], -jnp.inf))
        if i + 1 < n_tiles:
            bufs[cur][...] = scores(r0 + tq, tq, 0)
        acc_ref[par, bot, :] = alpha * acc_ref[par, bot, :] + pv
        o_ref[r0:r0 + tq, :] = (acc_ref[par] / l_ref[par]).astype(BF16)


def _attention(q, k, v, *, tq):
    bsz, nh, seq, _ = q.shape
    whole = lambda width: pl.BlockSpec((None, None, seq, width), lambda b, h: (b, h, 0, 0))
    return pl.pallas_call(
        functools.partial(_attn_kernel, tq=tq, tk=tq // 2, n_tiles=seq // tq),
        grid=(bsz, nh),
        in_specs=[whole(D_QK), whole(D_QK), whole(D_V)],
        out_specs=pl.BlockSpec((None, seq, D_V), lambda b, h: (b, 0, h)),
        out_shape=jax.ShapeDtypeStruct((bsz, seq, nh * D_V), BF16),
        scratch_shapes=[pltpu.VMEM((tq, tq // 2), F32), pltpu.VMEM((tq, tq // 2), F32),
                        pltpu.VMEM((2, tq, LANES), F32), pltpu.VMEM((2, tq, LANES), F32),
                        pltpu.VMEM((2, tq, D_V), F32)],
        compiler_params=pltpu.CompilerParams(dimension_semantics=("parallel", "parallel"),
                                             vmem_limit_bytes=VMEM_LIMIT_BYTES),
        name="attn",
    )(q, k, v)


def _ssd_kernel(xs_ref, bm_ref, cm_ref, dta_ref, z_ref, dsk_ref, gss_ref, o_ref,
                state_ref, rows_ref, *, n_sub):
    ci = pl.program_id(1)

    @pl.when(ci == 0)
    def _():
        state_ref[...] = jnp.zeros_like(state_ref)

    L = CHUNK
    r_i = lax.broadcasted_iota(jnp.int32, (L, L), 0)
    c_i = lax.broadcasted_iota(jnp.int32, (L, L), 1)
    tril = c_i <= r_i
    tril_f = tril.astype(F32)
    eye_b = (r_i == c_i).astype(BF16)
    lane = lax.broadcasted_iota(jnp.int32, (1, L), 1)
    in_a = (lane >= 32) & (lane < 64)
    first = lane < 64

    for sc in range(n_sub):
        rows = slice(sc * L, (sc + 1) * L)
        dta = dta_ref[rows, :]
        cum = jnp.dot(tril_f, dta, precision=HIGHEST, preferred_element_type=F32)
        acm2 = jnp.where(in_a, cum * LOG2E, 0.0)
        comb_t = jnp.where(in_a, acm2, dta).T
        dt_t = comb_t[0:SSM_HEADS]
        a2_t = comb_t[SSM_HEADS:2 * SSM_HEADS]
        rows_ref[sc, 0] = a2_t - jnp.log2(dt_t)
        rows_ref[sc, 1] = jnp.exp2(a2_t[:, L - 1:L] - a2_t) * dt_t
        cd_all = jnp.exp2(acm2[L - 1:L, :])

        cgs = [cm_ref[rows, g * SSM_N:(g + 1) * SSM_N] for g in range(SSM_GROUPS)]
        bgs = [bm_ref[rows, g * SSM_N:(g + 1) * SSM_N] for g in range(SSM_GROUPS)]
        cbs = [_dot_nt(cgs[g], bgs[g]) for g in range(SSM_GROUPS)]
        bts = [_dot_nt(eye_b, bgs[g]) for g in range(SSM_GROUPS)]

        def finish(g, parts):
            ys = []
            for pr, (yy, ss, s_old) in enumerate(parts):
                hp = g * 2 + pr
                ys.append(jnp.where(first, yy[0:L], yy[L:2 * L]))
                h0 = hp * 2
                cdrow = jnp.where(first, cd_all[:, 32 + h0:33 + h0], cd_all[:, 33 + h0:34 + h0])
                state_ref[hp] = cdrow * s_old + jnp.where(first, ss[0:SSM_N], ss[SSM_N:2 * SSM_N])
            sl = slice(g * 256, (g + 1) * 256)
            xg = xs_ref[rows, sl].astype(F32)
            y = jnp.concatenate(ys, axis=1) + dsk_ref[:, sl] * xg
            zg = z_ref[rows, sl].astype(F32)
            yz = y * (zg * _sigmoid(zg))
            o_ref[rows, sl] = _rms(yz, gss_ref[:, sl]).astype(BF16)

        pending = None
        for g in range(SSM_GROUPS):
            cg32 = cgs[g].astype(F32)
            parts = []
            for pr in range(2):
                hp = g * 2 + pr
                xpair = xs_ref[rows, hp * 128:(hp + 1) * 128]
                s_old = state_ref[hp]
                rhs = jnp.concatenate([xpair, s_old.astype(BF16)], axis=0)
                lhs_parts = []
                bw_parts = []
                for e in range(2):
                    hh = hp * 2 + e
                    colb = jnp.broadcast_to(acm2[:, 32 + hh:33 + hh], (L, L))
                    dec = jnp.exp2(jnp.where(tril, colb - rows_ref[sc, 0, hh:hh + 1, :], -jnp.inf))
                    mh = (cbs[g] * dec).astype(BF16)
                    ce = (cg32 * jnp.exp2(colb)).astype(BF16)
                    lhs_parts.append(jnp.concatenate([mh, ce], axis=1))
                    bw_parts.append((bts[g] * rows_ref[sc, 1, hh:hh + 1, :]).astype(BF16))
                yy = _dot(jnp.concatenate(lhs_parts, axis=0), rhs)
                ss = _dot(jnp.concatenate(bw_parts, axis=0), xpair)
                parts.append((yy, ss, s_old))
            if pending is not None:
                finish(*pending)
            pending = (g, parts)
        finish(*pending)


def _ssd(xs, bm, cm, dta, z, dsk, gss, *, n_sub):
    bsz, seq, _ = xs.shape
    n_bc = SSM_GROUPS * SSM_N
    rows = n_sub * CHUNK
    tok = lambda width: pl.BlockSpec((None, rows, width), lambda b, c: (b, c, 0))
    return pl.pallas_call(
        functools.partial(_ssd_kernel, n_sub=n_sub),
        grid=(bsz, seq // rows),
        in_specs=[tok(SSM_INNER), tok(n_bc), tok(n_bc), tok(LANES), tok(SSM_INNER),
                  _const_spec(dsk.shape), _const_spec(gss.shape)],
        out_specs=tok(SSM_INNER),
        out_shape=jax.ShapeDtypeStruct((bsz, seq, SSM_INNER), BF16),
        scratch_shapes=[pltpu.VMEM((SSM_HEADS // 2, SSM_N, 128), F32),
                        pltpu.VMEM((n_sub, 2, SSM_HEADS, CHUNK), F32)],
        compiler_params=pltpu.CompilerParams(dimension_semantics=("parallel", "arbitrary"),
                                             vmem_limit_bytes=VMEM_LIMIT_BYTES),
        name="ssd",
    )(xs, bm, cm, dta, z, dsk, gss)


def _back_kernel(x_ref, attn_ref, yg_ref, mod_ref, gpre1_ref, wga_ref, wgb_ref, woa_ref, wob_ref, wout_ref,
                 gpost1_ref, gpre2_ref, w1_ref, w2_ref, gpost2_ref, o_ref, mb_ref, acc_ref, *, cw, cw_ff):
    x = x_ref[...]
    shift1, scale1, gate1 = mod_ref[0:1, :], mod_ref[1:2, :], mod_ref[2:3, :]
    shift2, scale2, gate2 = mod_ref[3:4, :], mod_ref[4:5, :], mod_ref[5:6, :]
    hb = (_rms(x, gpre1_ref[...]) * (1.0 + scale1) + shift1).astype(BF16)
    at = attn_ref[...]
    yg = yg_ref[...]
    d = x.shape[1]
    for c0 in range(0, d, cw):
        sl = slice(c0, c0 + cw)
        ga = _sigmoid(_dot(hb, wga_ref[:, sl]))
        gb = _sigmoid(_dot(hb, wgb_ref[:, sl]))
        merged = ga * _dot(at, woa_ref[:, sl]) + gb * _dot(yg, wob_ref[:, sl])
        mb_ref[:, sl] = merged.astype(BF16)
    mix = _dot(mb_ref[...], wout_ref[...])
    o_ref[...] = x + gate1 * _rms(mix, gpost1_ref[...])
    hb2 = (_rms(o_ref[...], gpre2_ref[...]) * (1.0 + scale2) + shift2).astype(BF16)
    dff = w1_ref.shape[1]
    for ci, c0 in enumerate(range(0, dff, cw_ff)):
        a = jnp.maximum(_dot(hb2, w1_ref[:, c0:c0 + cw_ff]), 0.0)
        part = _dot((a * a).astype(BF16), w2_ref[c0:c0 + cw_ff, :])
        if ci == 0:
            acc_ref[...] = part
        else:
            acc_ref[...] += part
    o_ref[...] = o_ref[...] + gate2 * _rms(acc_ref[...], gpost2_ref[...])


def _back(x, attn, yg, mod8, gpre1, wga, wgb, woa, wob, wout, gpost1, gpre2, w1, w2, gpost2, *, tm):
    bsz, seq, d = x.shape
    tok = lambda width: pl.BlockSpec((None, tm, width), lambda b, i: (b, i, 0))
    consts = (gpre1, wga, wgb, woa, wob, wout, gpost1, gpre2, w1, w2, gpost2)
    return pl.pallas_call(
        functools.partial(_back_kernel, cw=512, cw_ff=1024),
        grid=(bsz, seq // tm),
        in_specs=[tok(d), tok(attn.shape[2]), tok(yg.shape[2]),
                  pl.BlockSpec((None, 8, d), lambda b, i: (b, 0, 0))] + [_const_spec(c.shape) for c in consts],
        out_specs=tok(d),
        out_shape=jax.ShapeDtypeStruct((bsz, seq, d), F32),
        scratch_shapes=[pltpu.VMEM((tm, d), BF16), pltpu.VMEM((tm, d), F32)],
        compiler_params=pltpu.CompilerParams(dimension_semantics=("parallel", "parallel"),
                                             vmem_limit_bytes=VMEM_LIMIT_BYTES),
        name="back",
    )(x, attn, yg, mod8, *consts)


def _pad_row(v, width, offset=0):
    out = jnp.zeros((1, width), F32)
    return out.at[0, offset:offset + v.shape[0]].set(v.astype(F32))


def kernel(x, c, positions, w_ada, b_ada, g_pre_mix, g_post_mix, w_in, g_q_lat, g_kv_lat, w_uq, w_ukv,
           w_o_attn, conv_w, conv_b, dt_bias, a_log, d_skip, g_ssm_out, w_o_ssm, w_out, g_pre_mlp,
           g_post_mlp, w_ff1, w_ff2):
    bsz, seq, d = x.shape
    depth = w_ada.shape[0]
    tm = min(512, seq)
    tq = min(1024, seq)
    half = D_ROPE // 2
    inv = ROPE_THETA ** (-jnp.arange(half, dtype=F32) / half)
    inv128 = jnp.broadcast_to(inv[:, None], (half, LANES))
    pos3 = positions.reshape(bsz, 1, seq)
    row = lambda v: v.reshape(1, -1).astype(F32)

    for l in range(depth):
        wi = w_in[l]
        o_q, o_kv, o_kr, o_z = 0, R_Q, R_Q + R_KV, R_Q + R_KV + D_ROPE
        o_xbc = o_z + SSM_INNER
        o_dt = o_xbc + SSM_INNER + 2 * SSM_GROUPS * SSM_N
        o_ga = o_dt + SSM_HEADS
        o_gb = o_ga + d
        kr1 = wi[:, o_kr:o_kr + half]
        kr2 = wi[:, o_kr + half:o_kr + D_ROPE]
        wdt = wi[:, o_dt:o_dt + SSM_HEADS]
        wlat = jnp.concatenate(
            [wi[:, o_q:o_kr], kr1, kr2, kr2, kr1, wdt, wdt, jnp.zeros((d, 64), F32)], axis=1).astype(BF16)
        wz = wi[:, o_z:o_xbc].astype(BF16)
        wxbc = wi[:, o_xbc:o_dt].astype(BF16)
        wga = wi[:, o_ga:o_gb].astype(BF16)
        wgb = wi[:, o_gb:o_gb + d].astype(BF16)
        wq3 = w_uq[l].reshape(R_Q, N_HEADS, D_QK)
        q1 = wq3[:, :, D_NOPE:D_NOPE + half]
        q2 = wq3[:, :, D_NOPE + half:]
        wq = jnp.concatenate([wq3[:, :, :D_NOPE], q1, q2, q2, q1], axis=2).transpose(1, 0, 2).astype(BF16)
        wkv = w_ukv[l].reshape(R_KV, N_HEADS, D_NOPE + D_V).transpose(1, 0, 2).astype(BF16)
        dtb = jnp.concatenate([row(dt_bias[l]), row(dt_bias[l]), jnp.zeros((1, 64), F32)], axis=1)
        alog = _pad_row(a_log[l], LANES, 32)
        dsk = jnp.repeat(d_skip[l].astype(F32), SSM_P).reshape(1, SSM_INNER)

        mod = _ada(c, w_ada[l], b_ada[l])
        mod8 = jnp.concatenate([mod.reshape(bsz, 6, d), jnp.zeros((bsz, 2, d), F32)], axis=1)

        q, k, v, z, xs, bm, cm, dta = _front(
            x, pos3, mod8, row(g_pre_mix[l]), wlat, row(g_q_lat[l]), row(g_kv_lat[l]), wq, wkv, wz, wxbc,
            conv_w[l].astype(F32), row(conv_b[l]), dtb, alog, inv128, tm=tm)
        attn = _attention(q, k, v, tq=tq)
        yg = _ssd(xs, bm, cm, dta, z, dsk, row(g_ssm_out[l]), n_sub=min(4, seq // CHUNK))
        x = _back(x, attn, yg, mod8, row(g_pre_mix[l]), wga, wgb, w_o_attn[l].astype(BF16),
                  w_o_ssm[l].astype(BF16), w_out[l].astype(BF16), row(g_post_mix[l]), row(g_pre_mlp[l]),
                  w_ff1[l].astype(BF16), w_ff2[l].astype(BF16), row(g_post_mlp[l]), tm=tm)
    return x
```

```python
import functools

import jax
import jax.numpy as jnp
from jax import lax
from jax.experimental import pallas as pl
from jax.experimental.pallas import tpu as pltpu

F32 = jnp.float32
BF16 = jnp.bfloat16
HIGHEST = lax.Precision.HIGHEST

N_HEADS = 8
D_NOPE = 128
D_ROPE = 64
D_V = 128
D_QK = D_NOPE + D_ROPE
R_Q = 256
R_KV = 256
ROPE_THETA = 10000.0
SSM_INNER = 2048
SSM_P = 64
SSM_HEADS = 32
SSM_GROUPS = 8
SSM_N = 128
CONV_K = 4
CHUNK = 128
EPS = 1e-6
LOG2E = 1.4426950408889634

VMEM_LIMIT_BYTES = 56 * 1024 * 1024
LANES = 128

_NT = (((1,), (1,)), ((), ()))


def _dot(a, b):
    return jnp.dot(a, b, preferred_element_type=F32)


def _dot_nt(a, b, precision=None):
    return lax.dot_general(a, b, _NT, precision=precision, preferred_element_type=F32)


def _rms(x, g):
    ms = jnp.mean(x * x, axis=-1, keepdims=True)
    return x * lax.rsqrt(ms + EPS) * g


def _sigmoid(x):
    return 1.0 / (1.0 + jnp.exp(-x))


def _const_spec(shape):
    n = len(shape)
    return pl.BlockSpec(shape, lambda *_: (0,) * n, pipeline_mode=pl.Buffered(1))


def _ada_kernel(c_ref, w_ref, b_ref, o_ref):
    c = c_ref[...]
    sc = c * _sigmoid(c)
    o_ref[...] = jnp.dot(sc, w_ref[...], precision=HIGHEST, preferred_element_type=F32) + b_ref[...]


def _ada(c, w_ada, b_ada):
    bsz, d = c.shape
    n = w_ada.shape[1]
    blk = 1024
    return pl.pallas_call(
        _ada_kernel,
        grid=(n // blk,),
        in_specs=[pl.BlockSpec((bsz, d), lambda j: (0, 0)),
                  pl.BlockSpec((d, blk), lambda j: (0, j)),
                  pl.BlockSpec((1, blk), lambda j: (0, j))],
        out_specs=pl.BlockSpec((bsz, blk), lambda j: (0, j)),
        out_shape=jax.ShapeDtypeStruct((bsz, n), F32),
        name="ada",
    )(c, w_ada, b_ada.reshape(1, n))


def _front_kernel(x_ref, pos_ref, mod_ref, gpre_ref, wlat_ref, gq_ref, gkv_ref, wq_ref, wkv_ref,
                  wz_ref, wxbc_ref, convw_ref, convb_ref, dtb_ref, alog_ref, inv_ref,
                  q_ref, k_ref, v_ref, z_ref, xs_ref, bm_ref, cm_ref, dta_ref,
                  carry_ref, cbuf_ref, tin_ref, tout_ref, *, tm, cw, qscale, pitch):
    i = pl.program_id(1)

    @pl.when(i == 0)
    def _():
        carry_ref[...] = jnp.zeros_like(carry_ref)

    x = x_ref[...]
    shift1 = mod_ref[0:1, :]
    scale1 = mod_ref[1:2, :]
    h = _rms(x, gpre_ref[...]) * (1.0 + scale1) + shift1
    hb = h.astype(BF16)

    lat = _dot(hb, wlat_ref[...])
    qn = _rms(lat[:, 0:R_Q], gq_ref[...]).astype(BF16)
    kvn = _rms(lat[:, R_Q:R_Q + R_KV], gkv_ref[...]).astype(BF16)
    kr = lat[:, 512:640]
    dtr = lat[:, 640:768]

    lane = lax.broadcasted_iota(jnp.int32, (1, LANES), 1)
    pre = dtr + dtb_ref[...]
    sp = jnp.maximum(pre, 0.0) + jnp.log1p(jnp.exp(-jnp.abs(pre)))
    arow = jnp.where(lane < 32, 1.0, jnp.where(lane < 64, -jnp.exp(alog_ref[...]), 0.0))
    dta_ref[...] = sp * arow

    ang = jnp.tile(inv_ref[...], (1, tm // LANES)) * pos_ref[...].astype(F32)
    cs = jnp.cos(ang)
    sn = jnp.sin(ang)
    rot = jnp.concatenate([cs, cs, -sn, sn], axis=0).T

    def rope(t):
        u = t * rot
        return u + pltpu.roll(u, 64, axis=1)

    kro = rope(kr)[:, 0:D_ROPE].astype(BF16)

    for hd in range(N_HEADS):
        qh = _dot(qn, wq_ref[hd])
        q_ref[hd, :, 0:D_NOPE] = (qh[:, 0:D_NOPE] * qscale).astype(BF16)
        q_ref[hd, :, D_NOPE:D_QK] = (rope(qh[:, 128:256])[:, 0:D_ROPE] * qscale).astype(BF16)
        kvh = _dot(kvn, wkv_ref[hd])
        k_ref[hd, :, 0:D_NOPE] = kvh[:, 0:D_NOPE].astype(BF16)
        k_ref[hd, :, D_NOPE:D_QK] = kro
        v_ref[hd] = kvh[:, 128:256].astype(BF16)

    d = x.shape[1]
    rows_p = 8 * pitch
    for j in range(d // LANES):
        tin_ref[j, 0:tm, :] = h[:, j * LANES:(j + 1) * LANES]
        tin_ref[j, tm:rows_p, :] = jnp.zeros((rows_p - tm, LANES), F32)
    hp = jnp.concatenate(
        [jnp.concatenate([tin_ref[j, pl.ds(a, 8, stride=pitch), :] for j in range(d // LANES)], axis=1)
         for a in range(pitch)], axis=0).astype(BF16)
    sub = lax.broadcasted_iota(jnp.int32, (8, cw), 0)
    n_xbc = wxbc_ref.shape[1]
    z_every = n_xbc // SSM_INNER
    for ci, c0 in enumerate(range(0, n_xbc, cw)):
        slot = ci % 2
        cols = slice(c0, c0 + cw)
        u = _dot(hp, wxbc_ref[:, cols])
        if ci % z_every == 0:
            zc = slice((ci // z_every) * cw, (ci // z_every + 1) * cw)
            z_ref[:, zc] = _dot(hb, wz_ref[:, zc]).astype(BF16)
        for j in range(1, CONV_K):
            grp = pltpu.roll(u[(pitch - j) * 8:(pitch - j + 1) * 8, :], 1, axis=0)
            prev = carry_ref[CONV_K - 1 - j:CONV_K - j, cols]
            cbuf_ref[slot, (CONV_K - 1 - j) * 8:(CONV_K - j) * 8, :] = jnp.where(sub == 0, prev, grp)
        hdr = (CONV_K - 1) * 8
        cbuf_ref[slot, hdr:hdr + rows_p, :] = u
        for j in range(1, CONV_K):
            t = tm - j
            r = (t % pitch) * 8 + t // pitch
            carry_ref[CONV_K - 1 - j:CONV_K - j, cols] = u[r:r + 1, :]
        acc = convb_ref[:, cols]
        for kk in range(CONV_K):
            acc = acc + cbuf_ref[slot, kk * 8:kk * 8 + rows_p, :] * convw_ref[kk:kk + 1, cols]
        for a in range(pitch):
            for jj in range(cw // LANES):
                tout_ref[slot, jj, pl.ds(a, 8, stride=pitch), :] = acc[a * 8:(a + 1) * 8,
                                                                      jj * LANES:(jj + 1) * LANES]
        v = jnp.concatenate([tout_ref[slot, jj, 0:tm, :] for jj in range(cw // LANES)], axis=1)
        sv = (v * _sigmoid(v)).astype(BF16)
        if c0 < SSM_INNER:
            xs_ref[:, cols] = sv
        elif c0 < SSM_INNER + SSM_GROUPS * SSM_N:
            o0 = c0 - SSM_INNER
            bm_ref[:, o0:o0 + cw] = sv
        else:
            o0 = c0 - SSM_INNER - SSM_GROUPS * SSM_N
            cm_ref[:, o0:o0 + cw] = sv


def _front(x, pos3, mod8, gpre, wlat, gq, gkv, wq, wkv, wz, wxbc, convw, convb, dtb, alog, inv128, *, tm):
    bsz, seq, d = x.shape
    cw = 256
    assert tm % 64 == 0
    pitch = tm // 8 + 4
    qscale = float(D_QK ** -0.5 * LOG2E)
    n_bc = SSM_GROUPS * SSM_N
    tok = lambda width: pl.BlockSpec((None, tm, width), lambda b, i: (b, i, 0))
    head = lambda width: pl.BlockSpec((None, N_HEADS, tm, width), lambda b, i: (b, 0, i, 0))
    in_specs = [
        tok(d),
        pl.BlockSpec((None, 1, tm), lambda b, i: (b, 0, i)),
        pl.BlockSpec((None, 8, d), lambda b, i: (b, 0, 0)),
        _const_spec(gpre.shape), _const_spec(wlat.shape), _const_spec(gq.shape), _const_spec(gkv.shape),
        _const_spec(wq.shape), _const_spec(wkv.shape), _const_spec(wz.shape), _const_spec(wxbc.shape),
        _const_spec(convw.shape), _const_spec(convb.shape), _const_spec(dtb.shape), _const_spec(alog.shape),
        _const_spec(inv128.shape),
    ]
    out_specs = [head(D_QK), head(D_QK), head(D_V), tok(SSM_INNER), tok(SSM_INNER), tok(n_bc), tok(n_bc),
                 tok(LANES)]
    out_shape = [
        jax.ShapeDtypeStruct((bsz, N_HEADS, seq, D_QK), BF16),
        jax.ShapeDtypeStruct((bsz, N_HEADS, seq, D_QK), BF16),
        jax.ShapeDtypeStruct((bsz, N_HEADS, seq, D_V), BF16),
        jax.ShapeDtypeStruct((bsz, seq, SSM_INNER), BF16),
        jax.ShapeDtypeStruct((bsz, seq, SSM_INNER), BF16),
        jax.ShapeDtypeStruct((bsz, seq, n_bc), BF16),
        jax.ShapeDtypeStruct((bsz, seq, n_bc), BF16),
        jax.ShapeDtypeStruct((bsz, seq, LANES), F32),
    ]
    return pl.pallas_call(
        functools.partial(_front_kernel, tm=tm, cw=cw, qscale=qscale, pitch=pitch),
        grid=(bsz, seq // tm),
        in_specs=in_specs,
        out_specs=out_specs,
        out_shape=out_shape,
        scratch_shapes=[pltpu.VMEM((8, wxbc.shape[1]), F32),
                        pltpu.VMEM((2, (CONV_K - 1) * 8 + 8 * pitch, cw), F32),
                        pltpu.VMEM((d // LANES, 8 * pitch, LANES), F32),
                        pltpu.VMEM((2, cw // LANES, 8 * pitch, LANES), F32)],
        compiler_params=pltpu.CompilerParams(dimension_semantics=("parallel", "arbitrary"),
                                             vmem_limit_bytes=VMEM_LIMIT_BYTES),
        name="front",
    )(x, pos3, mod8, gpre, wlat, gq, gkv, wq, wkv, wz, wxbc, convw, convb, dtb, alog, inv128)


def _attn_kernel(q_ref, k_ref, v_ref, o_ref, sa_ref, sb_ref, m_ref, l_ref, acc_ref, *, tq, tk, n_tiles):
    bufs = (sa_ref, sb_ref)
    row = lax.broadcasted_iota(jnp.int32, (tq, tk), 0)
    col = lax.broadcasted_iota(jnp.int32, (tq, tk), 1)
    row_b = lax.broadcasted_iota(jnp.int32, (tk, tk), 0)
    col_b = lax.broadcasted_iota(jnp.int32, (tk, tk), 1)

    def scores(r0, n_rows, k0, kw):
        return _dot_nt(q_ref[r0:r0 + n_rows, :], k_ref[k0:k0 + kw, :])

    def update(par, rows, k0, s):
        kw = s.shape[1]
        m_prev = m_ref[par, rows, :]
        m_new = jnp.maximum(m_prev, jnp.max(s, axis=-1, keepdims=True))
        alpha = jnp.exp2(m_prev - m_new)
        p = jnp.exp2(s - jnp.tile(m_new, (1, kw // LANES)))
        m_ref[par, rows, :] = m_new
        l_ref[par, rows, :] = alpha * l_ref[par, rows, :] + jnp.sum(p, axis=-1, keepdims=True)
        pv = _dot(p.astype(BF16), v_ref[k0:k0 + kw, :])
        return alpha, pv

    allrows = slice(0, tq)
    bot = slice(tk, tq)
    cur = 0
    bufs[cur][:, 0:tk] = scores(0, tq, 0, tk)
    for i in range(n_tiles):
        par = i % 2
        r0 = i * tq
        m_ref[par] = jnp.full((tq, LANES), -jnp.inf, F32)
        l_ref[par] = jnp.zeros((tq, LANES), F32)
        acc_ref[par] = jnp.zeros((tq, D_V), F32)
        for w in range(i):
            alpha, pv = update(par, allrows, w * tq, bufs[cur][...])
            if w + 1 < i:
                bufs[1 - cur][...] = scores(r0, tq, (w + 1) * tq, tq)
            else:
                bufs[1 - cur][:, 0:tk] = scores(r0, tq, r0, tk)
            acc_ref[par] = alpha * acc_ref[par] + pv
            cur = 1 - cur
        alpha, pv = update(par, allrows, r0, jnp.where(col <= row, bufs[cur][:, 0:tk], -jnp.inf))
        bufs[1 - cur][0:tk, 0:tk] = scores(r0 + tk, tk, r0 + tk, tk)
        acc_ref[par] = alpha * acc_ref[par] + pv
        alpha, pv = update(par, bot, r0 + tk, jnp.where(col_b <= row_b, bufs[1 - cur][0:tk, 0:tk], -jnp.inf))
        if i + 1 < n_tiles:
            bufs[cur][...] = scores(r0 + tq, tq, 0, tq)
        acc_ref[par, bot, :] = alpha * acc_ref[par, bot, :] + pv
        o_ref[r0:r0 + tq, :] = (acc_ref[par] / l_ref[par]).astype(BF16)


def _attention(q, k, v, *, tq):
    bsz, nh, seq, _ = q.shape
    whole = lambda width: pl.BlockSpec((None, None, seq, width), lambda b, h: (b, h, 0, 0))
    return pl.pallas_call(
        functools.partial(_attn_kernel, tq=tq, tk=tq // 2, n_tiles=seq // tq),
        grid=(bsz, nh),
        in_specs=[whole(D_QK), whole(D_QK), whole(D_V)],
        out_specs=pl.BlockSpec((None, seq, D_V), lambda b, h: (b, 0, h)),
        out_shape=jax.ShapeDtypeStruct((bsz, seq, nh * D_V), BF16),
        scratch_shapes=[pltpu.VMEM((tq, tq), F32), pltpu.VMEM((tq, tq), F32),
                        pltpu.VMEM((2, tq, LANES), F32), pltpu.VMEM((2, tq, LANES), F32),
                        pltpu.VMEM((2, tq, D_V), F32)],
        compiler_params=pltpu.CompilerParams(dimension_semantics=("parallel", "parallel"),
                                             vmem_limit_bytes=VMEM_LIMIT_BYTES),
        name="attn",
    )(q, k, v)


def _ssd_kernel(xs_ref, bm_ref, cm_ref, dta_ref, z_ref, dsk_ref, gss_ref, o_ref,
                state_ref, rows_ref, *, n_sub):
    ci = pl.program_id(1)

    @pl.when(ci == 0)
    def _():
        state_ref[...] = jnp.zeros_like(state_ref)

    L = CHUNK
    r_i = lax.broadcasted_iota(jnp.int32, (L, L), 0)
    c_i = lax.broadcasted_iota(jnp.int32, (L, L), 1)
    tril = c_i <= r_i
    tril_f = tril.astype(F32)
    eye_b = (r_i == c_i).astype(BF16)
    lane = lax.broadcasted_iota(jnp.int32, (1, L), 1)
    in_a = (lane >= 32) & (lane < 64)
    first = lane < 64

    for sc in range(n_sub):
        rows = slice(sc * L, (sc + 1) * L)
        dta = dta_ref[rows, :]
        cum = jnp.dot(tril_f, dta, precision=HIGHEST, preferred_element_type=F32)
        acm2 = jnp.where(in_a, cum * LOG2E, 0.0)
        comb_t = jnp.where(in_a, acm2, dta).T
        dt_t = comb_t[0:SSM_HEADS]
        a2_t = comb_t[SSM_HEADS:2 * SSM_HEADS]
        rows_ref[sc, 0] = a2_t - jnp.log2(dt_t)
        rows_ref[sc, 1] = jnp.exp2(a2_t[:, L - 1:L] - a2_t) * dt_t
        cd_all = jnp.exp2(acm2[L - 1:L, :])

        cgs = [cm_ref[rows, g * SSM_N:(g + 1) * SSM_N] for g in range(SSM_GROUPS)]
        bgs = [bm_ref[rows, g * SSM_N:(g + 1) * SSM_N] for g in range(SSM_GROUPS)]
        cbs = [_dot_nt(cgs[g], bgs[g]) for g in range(SSM_GROUPS)]
        bts = [_dot_nt(eye_b, bgs[g]) for g in range(SSM_GROUPS)]

        def finish(g, parts):
            ys = []
            for pr, (yy, ss, s_old) in enumerate(parts):
                hp = g * 2 + pr
                ys.append(jnp.where(first, yy[0:L], yy[L:2 * L]))
                h0 = hp * 2
                cdrow = jnp.where(first, cd_all[:, 32 + h0:33 + h0], cd_all[:, 33 + h0:34 + h0])
                state_ref[hp] = cdrow * s_old + jnp.where(first, ss[0:SSM_N], ss[SSM_N:2 * SSM_N])
            sl = slice(g * 256, (g + 1) * 256)
            xg = xs_ref[rows, sl].astype(F32)
            y = jnp.concatenate(ys, axis=1) + dsk_ref[:, sl] * xg
            zg = z_ref[rows, sl].astype(F32)
            yz = y * (zg * _sigmoid(zg))
            o_ref[rows, sl] = _rms(yz, gss_ref[:, sl]).astype(BF16)

        pending = None
        for g in range(SSM_GROUPS):
            cg32 = cgs[g].astype(F32)
            parts = []
            for pr in range(2):
                hp = g * 2 + pr
                xpair = xs_ref[rows, hp * 128:(hp + 1) * 128]
                s_old = state_ref[hp]
                rhs = jnp.concatenate([xpair, s_old.astype(BF16)], axis=0)
                lhs_parts = []
                bw_parts = []
                for e in range(2):
                    hh = hp * 2 + e
                    colb = jnp.broadcast_to(acm2[:, 32 + hh:33 + hh], (L, L))
                    dec = jnp.exp2(jnp.where(tril, colb - rows_ref[sc, 0, hh:hh + 1, :], -jnp.inf))
                    mh = (cbs[g] * dec).astype(BF16)
                    ce = (cg32 * jnp.exp2(colb)).astype(BF16)
                    lhs_parts.append(jnp.concatenate([mh, ce], axis=1))
                    bw_parts.append((bts[g] * rows_ref[sc, 1, hh:hh + 1, :]).astype(BF16))
                yy = _dot(jnp.concatenate(lhs_parts, axis=0), rhs)
                ss = _dot(jnp.concatenate(bw_parts, axis=0), xpair)
                parts.append((yy, ss, s_old))
            if pending is not None:
                finish(*pending)
            pending = (g, parts)
        finish(*pending)


def _ssd(xs, bm, cm, dta, z, dsk, gss, *, n_sub):
    bsz, seq, _ = xs.shape
    n_bc = SSM_GROUPS * SSM_N
    rows = n_sub * CHUNK
    tok = lambda width: pl.BlockSpec((None, rows, width), lambda b, c: (b, c, 0))
    return pl.pallas_call(
        functools.partial(_ssd_kernel, n_sub=n_sub),
        grid=(bsz, seq // rows),
        in_specs=[tok(SSM_INNER), tok(n_bc), tok(n_bc), tok(LANES), tok(SSM_INNER),
                  _const_spec(dsk.shape), _const_spec(gss.shape)],
        out_specs=tok(SSM_INNER),
        out_shape=jax.ShapeDtypeStruct((bsz, seq, SSM_INNER), BF16),
        scratch_shapes=[pltpu.VMEM((SSM_HEADS // 2, SSM_N, 128), F32),
                        pltpu.VMEM((n_sub, 2, SSM_HEADS, CHUNK), F32)],
        compiler_params=pltpu.CompilerParams(dimension_semantics=("parallel", "arbitrary"),
                                             vmem_limit_bytes=VMEM_LIMIT_BYTES),
        name="ssd",
    )(xs, bm, cm, dta, z, dsk, gss)


def _back_kernel(x_ref, attn_ref, yg_ref, mod_ref, gpre1_ref, wga_ref, wgb_ref, woa_ref, wob_ref, wout_ref,
                 gpost1_ref, gpre2_ref, w1_ref, w2_ref, gpost2_ref, o_ref, mb_ref, acc_ref, *, cw, cw_ff):
    x = x_ref[...]
    shift1, scale1, gate1 = mod_ref[0:1, :], mod_ref[1:2, :], mod_ref[2:3, :]
    shift2, scale2, gate2 = mod_ref[3:4, :], mod_ref[4:5, :], mod_ref[5:6, :]
    hb = (_rms(x, gpre1_ref[...]) * (1.0 + scale1) + shift1).astype(BF16)
    at = attn_ref[...]
    yg = yg_ref[...]
    d = x.shape[1]
    for c0 in range(0, d, cw):
        sl = slice(c0, c0 + cw)
        ga = _sigmoid(_dot(hb, wga_ref[:, sl]))
        gb = _sigmoid(_dot(hb, wgb_ref[:, sl]))
        merged = ga * _dot(at, woa_ref[:, sl]) + gb * _dot(yg, wob_ref[:, sl])
        mb_ref[:, sl] = merged.astype(BF16)
    mix = _dot(mb_ref[...], wout_ref[...])
    o_ref[...] = x + gate1 * _rms(mix, gpost1_ref[...])
    hb2 = (_rms(o_ref[...], gpre2_ref[...]) * (1.0 + scale2) + shift2).astype(BF16)
    dff = w1_ref.shape[1]
    for ci, c0 in enumerate(range(0, dff, cw_ff)):
        a = jnp.maximum(_dot(hb2, w1_ref[:, c0:c0 + cw_ff]), 0.0)
        part = _dot((a * a).astype(BF16), w2_ref[c0:c0 + cw_ff, :])
        if ci == 0:
            acc_ref[...] = part
        else:
            acc_ref[...] += part
    o_ref[...] = o_ref[...] + gate2 * _rms(acc_ref[...], gpost2_ref[...])


def _back(x, attn, yg, mod8, gpre1, wga, wgb, woa, wob, wout, gpost1, gpre2, w1, w2, gpost2, *, tm):
    bsz, seq, d = x.shape
    tok = lambda width: pl.BlockSpec((None, tm, width), lambda b, i: (b, i, 0))
    consts = (gpre1, wga, wgb, woa, wob, wout, gpost1, gpre2, w1, w2, gpost2)
    return pl.pallas_call(
        functools.partial(_back_kernel, cw=512, cw_ff=1024),
        grid=(bsz, seq // tm),
        in_specs=[tok(d), tok(attn.shape[2]), tok(yg.shape[2]),
                  pl.BlockSpec((None, 8, d), lambda b, i: (b, 0, 0))] + [_const_spec(c.shape) for c in consts],
        out_specs=tok(d),
        out_shape=jax.ShapeDtypeStruct((bsz, seq, d), F32),
        scratch_shapes=[pltpu.VMEM((tm, d), BF16), pltpu.VMEM((tm, d), F32)],
        compiler_params=pltpu.CompilerParams(dimension_semantics=("parallel", "parallel"),
                                             vmem_limit_bytes=VMEM_LIMIT_BYTES),
        name="back",
    )(x, attn, yg, mod8, *consts)


def _pad_row(v, width, offset=0):
    out = jnp.zeros((1, width), F32)
    return out.at[0, offset:offset + v.shape[0]].set(v.astype(F32))


def kernel(x, c, positions, w_ada, b_ada, g_pre_mix, g_post_mix, w_in, g_q_lat, g_kv_lat, w_uq, w_ukv,
           w_o_attn, conv_w, conv_b, dt_bias, a_log, d_skip, g_ssm_out, w_o_ssm, w_out, g_pre_mlp,
           g_post_mlp, w_ff1, w_ff2):
    bsz, seq, d = x.shape
    depth = w_ada.shape[0]
    tm = min(512, seq)
    tq = min(1024, seq)
    half = D_ROPE // 2
    inv = ROPE_THETA ** (-jnp.arange(half, dtype=F32) / half)
    inv128 = jnp.broadcast_to(inv[:, None], (half, LANES))
    pos3 = positions.reshape(bsz, 1, seq)
    row = lambda v: v.reshape(1, -1).astype(F32)

    for l in range(depth):
        wi = w_in[l]
        o_q, o_kv, o_kr, o_z = 0, R_Q, R_Q + R_KV, R_Q + R_KV + D_ROPE
        o_xbc = o_z + SSM_INNER
        o_dt = o_xbc + SSM_INNER + 2 * SSM_GROUPS * SSM_N
        o_ga = o_dt + SSM_HEADS
        o_gb = o_ga + d
        kr1 = wi[:, o_kr:o_kr + half]
        kr2 = wi[:, o_kr + half:o_kr + D_ROPE]
        wdt = wi[:, o_dt:o_dt + SSM_HEADS]
        wlat = jnp.concatenate(
            [wi[:, o_q:o_kr], kr1, kr2, kr2, kr1, wdt, wdt, jnp.zeros((d, 64), F32)], axis=1).astype(BF16)
        wz = wi[:, o_z:o_xbc].astype(BF16)
        wxbc = wi[:, o_xbc:o_dt].astype(BF16)
        wga = wi[:, o_ga:o_gb].astype(BF16)
        wgb = wi[:, o_gb:o_gb + d].astype(BF16)
        wq3 = w_uq[l].reshape(R_Q, N_HEADS, D_QK)
        q1 = wq3[:, :, D_NOPE:D_NOPE + half]
        q2 = wq3[:, :, D_NOPE + half:]
        wq = jnp.concatenate([wq3[:, :, :D_NOPE], q1, q2, q2, q1], axis=2).transpose(1, 0, 2).astype(BF16)
        wkv = w_ukv[l].reshape(R_KV, N_HEADS, D_NOPE + D_V).transpose(1, 0, 2).astype(BF16)
        dtb = jnp.concatenate([row(dt_bias[l]), row(dt_bias[l]), jnp.zeros((1, 64), F32)], axis=1)
        alog = _pad_row(a_log[l], LANES, 32)
        dsk = jnp.repeat(d_skip[l].astype(F32), SSM_P).reshape(1, SSM_INNER)

        mod = _ada(c, w_ada[l], b_ada[l])
        mod8 = jnp.concatenate([mod.reshape(bsz, 6, d), jnp.zeros((bsz, 2, d), F32)], axis=1)

        q, k, v, z, xs, bm, cm, dta = _front(
            x, pos3, mod8, row(g_pre_mix[l]), wlat, row(g_q_lat[l]), row(g_kv_lat[l]), wq, wkv, wz, wxbc,
            conv_w[l].astype(F32), row(conv_b[l]), dtb, alog, inv128, tm=tm)
        attn = _attention(q, k, v, tq=tq)
        yg = _ssd(xs, bm, cm, dta, z, dsk, row(g_ssm_out[l]), n_sub=min(8, seq // CHUNK))
        x = _back(x, attn, yg, mod8, row(g_pre_mix[l]), wga, wgb, w_o_attn[l].astype(BF16),
                  w_o_ssm[l].astype(BF16), w_out[l].astype(BF16), row(g_post_mix[l]), row(g_pre_mlp[l]),
                  w_ff1[l].astype(BF16), w_ff2[l].astype(BF16), row(g_post_mlp[l]), tm=tm)
    return x
```

```python
import functools

import jax
import jax.numpy as jnp
from jax import lax
from jax.experimental import pallas as pl
from jax.experimental.pallas import tpu as pltpu

F32 = jnp.float32
BF16 = jnp.bfloat16
HIGHEST = lax.Precision.HIGHEST

N_HEADS = 8
D_NOPE = 128
D_ROPE = 64
D_V = 128
D_QK = D_NOPE + D_ROPE
R_Q = 256
R_KV = 256
ROPE_THETA = 10000.0
SSM_INNER = 2048
SSM_P = 64
SSM_HEADS = 32
SSM_GROUPS = 8
SSM_N = 128
CONV_K = 4
CHUNK = 128
EPS = 1e-6
LOG2E = 1.4426950408889634

VMEM_LIMIT_BYTES = 56 * 1024 * 1024
LANES = 128

_NT = (((1,), (1,)), ((), ()))


def _dot(a, b):
    return jnp.dot(a, b, preferred_element_type=F32)


def _dot_nt(a, b, precision=None):
    return lax.dot_general(a, b, _NT, precision=precision, preferred_element_type=F32)


def _rms(x, g):
    ms = jnp.mean(x * x, axis=-1, keepdims=True)
    return x * lax.rsqrt(ms + EPS) * g


def _sigmoid(x):
    return 1.0 / (1.0 + jnp.exp(-x))


def _const_spec(shape):
    n = len(shape)
    return pl.BlockSpec(shape, lambda *_: (0,) * n, pipeline_mode=pl.Buffered(1))


def _ada_kernel(c_ref, w_ref, b_ref, o_ref):
    c = c_ref[...]
    sc = c * _sigmoid(c)
    o_ref[...] = jnp.dot(sc, w_ref[...], precision=HIGHEST, preferred_element_type=F32) + b_ref[...]


def _ada(c, w_ada, b_ada):
    bsz, d = c.shape
    n = w_ada.shape[1]
    blk = 1024
    return pl.pallas_call(
        _ada_kernel,
        grid=(n // blk,),
        in_specs=[pl.BlockSpec((bsz, d), lambda j: (0, 0)),
                  pl.BlockSpec((d, blk), lambda j: (0, j)),
                  pl.BlockSpec((1, blk), lambda j: (0, j))],
        out_specs=pl.BlockSpec((bsz, blk), lambda j: (0, j)),
        out_shape=jax.ShapeDtypeStruct((bsz, n), F32),
        name="ada",
    )(c, w_ada, b_ada.reshape(1, n))


def _front_kernel(x_ref, pos_ref, mod_ref, gpre_ref, wlat_ref, gq_ref, gkv_ref, wq_ref, wkv_ref,
                  wz_ref, wxbc_ref, convw_ref, convb_ref, dtb_ref, alog_ref, inv_ref,
                  q_ref, k_ref, v_ref, z_ref, xs_ref, bm_ref, cm_ref, dta_ref,
                  carry_ref, cbuf_ref, tin_ref, tout_ref, *, tm, cw, qscale, pitch):
    i = pl.program_id(1)

    @pl.when(i == 0)
    def _():
        carry_ref[...] = jnp.zeros_like(carry_ref)

    x = x_ref[...]
    shift1 = mod_ref[0:1, :]
    scale1 = mod_ref[1:2, :]
    h = _rms(x, gpre_ref[...]) * (1.0 + scale1) + shift1
    hb = h.astype(BF16)

    lat = _dot(hb, wlat_ref[...])
    qn = _rms(lat[:, 0:R_Q], gq_ref[...]).astype(BF16)
    kvn = _rms(lat[:, R_Q:R_Q + R_KV], gkv_ref[...]).astype(BF16)
    kr = lat[:, 512:640]
    dtr = lat[:, 640:768]

    lane = lax.broadcasted_iota(jnp.int32, (1, LANES), 1)
    pre = dtr + dtb_ref[...]
    sp = jnp.maximum(pre, 0.0) + jnp.log1p(jnp.exp(-jnp.abs(pre)))
    arow = jnp.where(lane < 32, 1.0, jnp.where(lane < 64, -jnp.exp(alog_ref[...]), 0.0))
    dta_ref[...] = sp * arow

    ang = jnp.tile(inv_ref[...], (1, tm // LANES)) * pos_ref[...].astype(F32)
    cs = jnp.cos(ang)
    sn = jnp.sin(ang)
    rot = jnp.concatenate([cs, cs, -sn, sn], axis=0).T

    def rope(t):
        u = t * rot
        return u + pltpu.roll(u, 64, axis=1)

    kro = rope(kr)[:, 0:D_ROPE].astype(BF16)

    for hd in range(N_HEADS):
        qh = _dot(qn, wq_ref[hd])
        q_ref[hd, :, 0:D_NOPE] = (qh[:, 0:D_NOPE] * qscale).astype(BF16)
        q_ref[hd, :, D_NOPE:D_QK] = (rope(qh[:, 128:256])[:, 0:D_ROPE] * qscale).astype(BF16)
        kvh = _dot(kvn, wkv_ref[hd])
        k_ref[hd, :, 0:D_NOPE] = kvh[:, 0:D_NOPE].astype(BF16)
        k_ref[hd, :, D_NOPE:D_QK] = kro
        v_ref[hd] = kvh[:, 128:256].astype(BF16)

    d = x.shape[1]
    rows_p = 8 * pitch
    for j in range(d // LANES):
        tin_ref[j, 0:tm, :] = h[:, j * LANES:(j + 1) * LANES]
        tin_ref[j, tm:rows_p, :] = jnp.zeros((rows_p - tm, LANES), F32)
    hp = jnp.concatenate(
        [jnp.concatenate([tin_ref[j, pl.ds(a, 8, stride=pitch), :] for j in range(d // LANES)], axis=1)
         for a in range(pitch)], axis=0).astype(BF16)
    sub = lax.broadcasted_iota(jnp.int32, (8, cw), 0)
    n_xbc = wxbc_ref.shape[1]
    z_every = n_xbc // SSM_INNER
    for ci, c0 in enumerate(range(0, n_xbc, cw)):
        slot = ci % 2
        cols = slice(c0, c0 + cw)
        u = _dot(hp, wxbc_ref[:, cols])
        if ci % z_every == 0:
            zc = slice((ci // z_every) * cw, (ci // z_every + 1) * cw)
            z_ref[:, zc] = _dot(hb, wz_ref[:, zc]).astype(BF16)
        for j in range(1, CONV_K):
            grp = pltpu.roll(u[(pitch - j) * 8:(pitch - j + 1) * 8, :], 1, axis=0)
            prev = carry_ref[CONV_K - 1 - j:CONV_K - j, cols]
            cbuf_ref[slot, (CONV_K - 1 - j) * 8:(CONV_K - j) * 8, :] = jnp.where(sub == 0, prev, grp)
        hdr = (CONV_K - 1) * 8
        cbuf_ref[slot, hdr:hdr + rows_p, :] = u
        for j in range(1, CONV_K):
            t = tm - j
            r = (t % pitch) * 8 + t // pitch
            carry_ref[CONV_K - 1 - j:CONV_K - j, cols] = u[r:r + 1, :]
        acc = convb_ref[:, cols]
        for kk in range(CONV_K):
            acc = acc + cbuf_ref[slot, kk * 8:kk * 8 + rows_p, :] * convw_ref[kk:kk + 1, cols]
        for a in range(pitch):
            for jj in range(cw // LANES):
                tout_ref[slot, jj, pl.ds(a, 8, stride=pitch), :] = acc[a * 8:(a + 1) * 8,
                                                                      jj * LANES:(jj + 1) * LANES]
        v = jnp.concatenate([tout_ref[slot, jj, 0:tm, :] for jj in range(cw // LANES)], axis=1)
        sv = (v * _sigmoid(v)).astype(BF16)
        if c0 < SSM_INNER:
            xs_ref[:, cols] = sv
        elif c0 < SSM_INNER + SSM_GROUPS * SSM_N:
            o0 = c0 - SSM_INNER
            bm_ref[:, o0:o0 + cw] = sv
        else:
            o0 = c0 - SSM_INNER - SSM_GROUPS * SSM_N
            cm_ref[:, o0:o0 + cw] = sv


def _front(x, pos3, mod8, gpre, wlat, gq, gkv, wq, wkv, wz, wxbc, convw, convb, dtb, alog, inv128, *, tm):
    bsz, seq, d = x.shape
    cw = 256
    assert tm % 64 == 0
    pitch = tm // 8 + 4
    qscale = float(D_QK ** -0.5 * LOG2E)
    n_bc = SSM_GROUPS * SSM_N
    tok = lambda width: pl.BlockSpec((None, tm, width), lambda b, i: (b, i, 0))
    head = lambda width: pl.BlockSpec((None, N_HEADS, tm, width), lambda b, i: (b, 0, i, 0))
    in_specs = [
        tok(d),
        pl.BlockSpec((None, 1, tm), lambda b, i: (b, 0, i)),
        pl.BlockSpec((None, 8, d), lambda b, i: (b, 0, 0)),
        _const_spec(gpre.shape), _const_spec(wlat.shape), _const_spec(gq.shape), _const_spec(gkv.shape),
        _const_spec(wq.shape), _const_spec(wkv.shape), _const_spec(wz.shape), _const_spec(wxbc.shape),
        _const_spec(convw.shape), _const_spec(convb.shape), _const_spec(dtb.shape), _const_spec(alog.shape),
        _const_spec(inv128.shape),
    ]
    out_specs = [head(D_QK), head(D_QK), head(D_V), tok(SSM_INNER), tok(SSM_INNER), tok(n_bc), tok(n_bc),
                 tok(LANES)]
    out_shape = [
        jax.ShapeDtypeStruct((bsz, N_HEADS, seq, D_QK), BF16),
        jax.ShapeDtypeStruct((bsz, N_HEADS, seq, D_QK), BF16),
        jax.ShapeDtypeStruct((bsz, N_HEADS, seq, D_V), BF16),
        jax.ShapeDtypeStruct((bsz, seq, SSM_INNER), BF16),
        jax.ShapeDtypeStruct((bsz, seq, SSM_INNER), BF16),
        jax.ShapeDtypeStruct((bsz, seq, n_bc), BF16),
        jax.ShapeDtypeStruct((bsz, seq, n_bc), BF16),
        jax.ShapeDtypeStruct((bsz, seq, LANES), F32),
    ]
    return pl.pallas_call(
        functools.partial(_front_kernel, tm=tm, cw=cw, qscale=qscale, pitch=pitch),
        grid=(bsz, seq // tm),
        in_specs=in_specs,
        out_specs=out_specs,
        out_shape=out_shape,
        scratch_shapes=[pltpu.VMEM((8, wxbc.shape[1]), F32),
                        pltpu.VMEM((2, (CONV_K - 1) * 8 + 8 * pitch, cw), F32),
                        pltpu.VMEM((d // LANES, 8 * pitch, LANES), F32),
                        pltpu.VMEM((2, cw // LANES, 8 * pitch, LANES), F32)],
        compiler_params=pltpu.CompilerParams(dimension_semantics=("parallel", "arbitrary"),
                                             vmem_limit_bytes=VMEM_LIMIT_BYTES),
        name="front",
    )(x, pos3, mod8, gpre, wlat, gq, gkv, wq, wkv, wz, wxbc, convw, convb, dtb, alog, inv128)


def _attn_kernel(q_ref, k_ref, v_ref, o_ref, sa_ref, sb_ref, m_ref, l_ref, acc_ref, *, tq, tk, n_tiles):
    bufs = (sa_ref, sb_ref)
    row = lax.broadcasted_iota(jnp.int32, (tq, tk), 0)
    col = lax.broadcasted_iota(jnp.int32, (tq, tk), 1)
    row_b = lax.broadcasted_iota(jnp.int32, (tk, tk), 0)
    col_b = lax.broadcasted_iota(jnp.int32, (tk, tk), 1)

    def scores(r0, n_rows, k0, kw):
        return _dot_nt(q_ref[r0:r0 + n_rows, :], k_ref[k0:k0 + kw, :])

    def update(par, rows, k0, s, first=False):
        n, kw = s.shape
        s_max = jnp.max(s, axis=-1, keepdims=True)
        if first:
            m_new = jnp.broadcast_to(s_max, (n, LANES))
            alpha = None
        else:
            m_prev = m_ref[par, rows, :]
            m_new = jnp.maximum(m_prev, s_max)
            alpha = jnp.exp2(m_prev - m_new)
        p = jnp.exp2(s - jnp.tile(m_new, (1, kw // LANES)))
        p_sum = jnp.sum(p, axis=-1, keepdims=True)
        m_ref[par, rows, :] = m_new
        l_ref[par, rows, :] = jnp.broadcast_to(p_sum, (n, LANES)) if first else alpha * l_ref[par, rows, :] + p_sum
        pv = _dot(p.astype(BF16), v_ref[k0:k0 + kw, :])
        return alpha, pv

    def accumulate(par, rows, alpha, pv):
        acc_ref[par, rows, :] = pv if alpha is None else alpha * acc_ref[par, rows, :] + pv

    allrows = slice(0, tq)
    bot = slice(tk, tq)
    cur = 0
    bufs[cur][:, 0:tk] = scores(0, tq, 0, tk)
    for i in range(n_tiles):
        par = i % 2
        r0 = i * tq
        for w in range(i):
            alpha, pv = update(par, allrows, w * tq, bufs[cur][...], first=(w == 0))
            if w + 1 < i:
                bufs[1 - cur][...] = scores(r0, tq, (w + 1) * tq, tq)
            else:
                bufs[1 - cur][:, 0:tk] = scores(r0, tq, r0, tk)
            accumulate(par, allrows, alpha, pv)
            cur = 1 - cur
        alpha, pv = update(par, allrows, r0, jnp.where(col <= row, bufs[cur][:, 0:tk], -jnp.inf), first=(i == 0))
        bufs[1 - cur][0:tk, 0:tk] = scores(r0 + tk, tk, r0 + tk, tk)
        accumulate(par, allrows, alpha, pv)
        alpha, pv = update(par, bot, r0 + tk, jnp.where(col_b <= row_b, bufs[1 - cur][0:tk, 0:tk], -jnp.inf))
        if i + 1 < n_tiles:
            bufs[cur][...] = scores(r0 + tq, tq, 0, tq)
        accumulate(par, bot, alpha, pv)
        o_ref[r0:r0 + tq, :] = (acc_ref[par] / l_ref[par]).astype(BF16)


def _attention(q, k, v, *, tq):
    bsz, nh, seq, _ = q.shape
    whole = lambda width: pl.BlockSpec((None, None, seq, width), lambda b, h: (b, h, 0, 0))
    return pl.pallas_call(
        functools.partial(_attn_kernel, tq=tq, tk=tq // 2, n_tiles=seq // tq),
        grid=(bsz, nh),
        in_specs=[whole(D_QK), whole(D_QK), whole(D_V)],
        out_specs=pl.BlockSpec((None, seq, D_V), lambda b, h: (b, 0, h)),
        out_shape=jax.ShapeDtypeStruct((bsz, seq, nh * D_V), BF16),
        scratch_shapes=[pltpu.VMEM((tq, tq), F32), pltpu.VMEM((tq, tq), F32),
                        pltpu.VMEM((2, tq, LANES), F32), pltpu.VMEM((2, tq, LANES), F32),
                        pltpu.VMEM((2, tq, D_V), F32)],
        compiler_params=pltpu.CompilerParams(dimension_semantics=("parallel", "parallel"),
                                             vmem_limit_bytes=VMEM_LIMIT_BYTES),
        name="attn",
    )(q, k, v)


def _ssd_kernel(xs_ref, bm_ref, cm_ref, dta_ref, z_ref, dsk_ref, gss_ref, o_ref,
                state_ref, rows_ref, *, n_sub):
    ci = pl.program_id(1)

    @pl.when(ci == 0)
    def _():
        state_ref[...] = jnp.zeros_like(state_ref)

    L = CHUNK
    r_i = lax.broadcasted_iota(jnp.int32, (L, L), 0)
    c_i = lax.broadcasted_iota(jnp.int32, (L, L), 1)
    tril = c_i <= r_i
    tril_f = tril.astype(F32)
    eye_b = (r_i == c_i).astype(BF16)
    lane = lax.broadcasted_iota(jnp.int32, (1, L), 1)
    in_a = (lane >= 32) & (lane < 64)
    first = lane < 64

    for sc in range(n_sub):
        rows = slice(sc * L, (sc + 1) * L)
        dta = dta_ref[rows, :]
        cum = jnp.dot(tril_f, dta, precision=HIGHEST, preferred_element_type=F32)
        acm2 = jnp.where(in_a, cum * LOG2E, 0.0)
        comb_t = jnp.where(in_a, acm2, dta).T
        dt_t = comb_t[0:SSM_HEADS]
        a2_t = comb_t[SSM_HEADS:2 * SSM_HEADS]
        rows_ref[sc, 0] = a2_t - jnp.log2(dt_t)
        rows_ref[sc, 1] = jnp.exp2(a2_t[:, L - 1:L] - a2_t) * dt_t
        cd_all = jnp.exp2(acm2[L - 1:L, :])

        cgs = [cm_ref[rows, g * SSM_N:(g + 1) * SSM_N] for g in range(SSM_GROUPS)]
        bgs = [bm_ref[rows, g * SSM_N:(g + 1) * SSM_N] for g in range(SSM_GROUPS)]
        cbs = [_dot_nt(cgs[g], bgs[g]) for g in range(SSM_GROUPS)]
        bts = [_dot_nt(eye_b, bgs[g]) for g in range(SSM_GROUPS)]

        def finish(g, parts):
            ys = []
            for pr, (yy, ss, s_old) in enumerate(parts):
                hp = g * 2 + pr
                ys.append(jnp.where(first, yy[0:L], yy[L:2 * L]))
                h0 = hp * 2
                cdrow = jnp.where(first, cd_all[:, 32 + h0:33 + h0], cd_all[:, 33 + h0:34 + h0])
                state_ref[hp] = cdrow * s_old + jnp.where(first, ss[0:SSM_N], ss[SSM_N:2 * SSM_N])
            sl = slice(g * 256, (g + 1) * 256)
            xg = xs_ref[rows, sl].astype(F32)
            y = jnp.concatenate(ys, axis=1) + dsk_ref[:, sl] * xg
            zg = z_ref[rows, sl].astype(F32)
            yz = y * (zg * _sigmoid(zg))
            o_ref[rows, sl] = _rms(yz, gss_ref[:, sl]).astype(BF16)

        pending = None
        for g in range(SSM_GROUPS):
            cg32 = cgs[g].astype(F32)
            parts = []
            for pr in range(2):
                hp = g * 2 + pr
                xpair = xs_ref[rows, hp * 128:(hp + 1) * 128]
                s_old = state_ref[hp]
                rhs = jnp.concatenate([xpair, s_old.astype(BF16)], axis=0)
                lhs_parts = []
                bw_parts = []
                for e in range(2):
                    hh = hp * 2 + e
                    colb = jnp.broadcast_to(acm2[:, 32 + hh:33 + hh], (L, L))
                    dec = jnp.exp2(jnp.where(tril, colb - rows_ref[sc, 0, hh:hh + 1, :], -jnp.inf))
                    mh = (cbs[g] * dec).astype(BF16)
                    ce = (cg32 * jnp.exp2(colb)).astype(BF16)
                    lhs_parts.append(jnp.concatenate([mh, ce], axis=1))
                    bw_parts.append((bts[g] * rows_ref[sc, 1, hh:hh + 1, :]).astype(BF16))
                yy = _dot(jnp.concatenate(lhs_parts, axis=0), rhs)
                ss = _dot(jnp.concatenate(bw_parts, axis=0), xpair)
                parts.append((yy, ss, s_old))
            if pending is not None:
                finish(*pending)
            pending = (g, parts)
        finish(*pending)


def _ssd(xs, bm, cm, dta, z, dsk, gss, *, n_sub):
    bsz, seq, _ = xs.shape
    n_bc = SSM_GROUPS * SSM_N
    rows = n_sub * CHUNK
    tok = lambda width: pl.BlockSpec((None, rows, width), lambda b, c: (b, c, 0))
    return pl.pallas_call(
        functools.partial(_ssd_kernel, n_sub=n_sub),
        grid=(bsz, seq // rows),
        in_specs=[tok(SSM_INNER), tok(n_bc), tok(n_bc), tok(LANES), tok(SSM_INNER),
                  _const_spec(dsk.shape), _const_spec(gss.shape)],
        out_specs=tok(SSM_INNER),
        out_shape=jax.ShapeDtypeStruct((bsz, seq, SSM_INNER), BF16),
        scratch_shapes=[pltpu.VMEM((SSM_HEADS // 2, SSM_N, 128), F32),
                        pltpu.VMEM((n_sub, 2, SSM_HEADS, CHUNK), F32)],
        compiler_params=pltpu.CompilerParams(dimension_semantics=("parallel", "arbitrary"),
                                             vmem_limit_bytes=VMEM_LIMIT_BYTES),
        name="ssd",
    )(xs, bm, cm, dta, z, dsk, gss)


def _back_kernel(x_ref, attn_ref, yg_ref, mod_ref, gpre1_ref, wga_ref, wgb_ref, woa_ref, wob_ref, wout_ref,
                 gpost1_ref, gpre2_ref, w1_ref, w2_ref, gpost2_ref, o_ref, mb_ref, acc_ref, *, cw, cw_ff):
    x = x_ref[...]
    shift1, scale1, gate1 = mod_ref[0:1, :], mod_ref[1:2, :], mod_ref[2:3, :]
    shift2, scale2, gate2 = mod_ref[3:4, :], mod_ref[4:5, :], mod_ref[5:6, :]
    hb = (_rms(x, gpre1_ref[...]) * (1.0 + scale1) + shift1).astype(BF16)
    at = attn_ref[...]
    yg = yg_ref[...]
    d = x.shape[1]
    for c0 in range(0, d, cw):
        sl = slice(c0, c0 + cw)
        ga = _sigmoid(_dot(hb, wga_ref[:, sl]))
        gb = _sigmoid(_dot(hb, wgb_ref[:, sl]))
        merged = ga * _dot(at, woa_ref[:, sl]) + gb * _dot(yg, wob_ref[:, sl])
        mb_ref[:, sl] = merged.astype(BF16)
    mix = _dot(mb_ref[...], wout_ref[...])
    o_ref[...] = x + gate1 * _rms(mix, gpost1_ref[...])
    hb2 = (_rms(o_ref[...], gpre2_ref[...]) * (1.0 + scale2) + shift2).astype(BF16)
    dff = w1_ref.shape[1]
    for ci, c0 in enumerate(range(0, dff, cw_ff)):
        a = jnp.maximum(_dot(hb2, w1_ref[:, c0:c0 + cw_ff]), 0.0)
        part = _dot((a * a).astype(BF16), w2_ref[c0:c0 + cw_ff, :])
        if ci == 0:
            acc_ref[...] = part
        else:
            acc_ref[...] += part
    o_ref[...] = o_ref[...] + gate2 * _rms(acc_ref[...], gpost2_ref[...])


def _back(x, attn, yg, mod8, gpre1, wga, wgb, woa, wob, wout, gpost1, gpre2, w1, w2, gpost2, *, tm):
    bsz, seq, d = x.shape
    tok = lambda width: pl.BlockSpec((None, tm, width), lambda b, i: (b, i, 0))
    consts = (gpre1, wga, wgb, woa, wob, wout, gpost1, gpre2, w1, w2, gpost2)
    return pl.pallas_call(
        functools.partial(_back_kernel, cw=512, cw_ff=1024),
        grid=(bsz, seq // tm),
        in_specs=[tok(d), tok(attn.shape[2]), tok(yg.shape[2]),
                  pl.BlockSpec((None, 8, d), lambda b, i: (b, 0, 0))] + [_const_spec(c.shape) for c in consts],
        out_specs=tok(d),
        out_shape=jax.ShapeDtypeStruct((bsz, seq, d), F32),
        scratch_shapes=[pltpu.VMEM((tm, d), BF16), pltpu.VMEM((tm, d), F32)],
        compiler_params=pltpu.CompilerParams(dimension_semantics=("parallel", "parallel"),
                                             vmem_limit_bytes=VMEM_LIMIT_BYTES),
        name="back",
    )(x, attn, yg, mod8, *consts)


def _pad_row(v, width, offset=0):
    out = jnp.zeros((1, width), F32)
    return out.at[0, offset:offset + v.shape[0]].set(v.astype(F32))


def kernel(x, c, positions, w_ada, b_ada, g_pre_mix, g_post_mix, w_in, g_q_lat, g_kv_lat, w_uq, w_ukv,
           w_o_attn, conv_w, conv_b, dt_bias, a_log, d_skip, g_ssm_out, w_o_ssm, w_out, g_pre_mlp,
           g_post_mlp, w_ff1, w_ff2):
    bsz, seq, d = x.shape
    depth = w_ada.shape[0]
    tm = min(512, seq)
    tq = min(1024, seq)
    half = D_ROPE // 2
    inv = ROPE_THETA ** (-jnp.arange(half, dtype=F32) / half)
    inv128 = jnp.broadcast_to(inv[:, None], (half, LANES))
    pos3 = positions.reshape(bsz, 1, seq)
    row = lambda v: v.reshape(1, -1).astype(F32)

    for l in range(depth):
        wi = w_in[l]
        o_q, o_kv, o_kr, o_z = 0, R_Q, R_Q + R_KV, R_Q + R_KV + D_ROPE
        o_xbc = o_z + SSM_INNER
        o_dt = o_xbc + SSM_INNER + 2 * SSM_GROUPS * SSM_N
        o_ga = o_dt + SSM_HEADS
        o_gb = o_ga + d
        kr1 = wi[:, o_kr:o_kr + half]
        kr2 = wi[:, o_kr + half:o_kr + D_ROPE]
        wdt = wi[:, o_dt:o_dt + SSM_HEADS]
        wlat = jnp.concatenate(
            [wi[:, o_q:o_kr], kr1, kr2, kr2, kr1, wdt, wdt, jnp.zeros((d, 64), F32)], axis=1).astype(BF16)
        wz = wi[:, o_z:o_xbc].astype(BF16)
        wxbc = wi[:, o_xbc:o_dt].astype(BF16)
        wga = wi[:, o_ga:o_gb].astype(BF16)
        wgb = wi[:, o_gb:o_gb + d].astype(BF16)
        wq3 = w_uq[l].reshape(R_Q, N_HEADS, D_QK)
        q1 = wq3[:, :, D_NOPE:D_NOPE + half]
        q2 = wq3[:, :, D_NOPE + half:]
        wq = jnp.concatenate([wq3[:, :, :D_NOPE], q1, q2, q2, q1], axis=2).transpose(1, 0, 2).astype(BF16)
        wkv = w_ukv[l].reshape(R_KV, N_HEADS, D_NOPE + D_V).transpose(1, 0, 2).astype(BF16)
        dtb = jnp.concatenate([row(dt_bias[l]), row(dt_bias[l]), jnp.zeros((1, 64), F32)], axis=1)
        alog = _pad_row(a_log[l], LANES, 32)
        dsk = jnp.repeat(d_skip[l].astype(F32), SSM_P).reshape(1, SSM_INNER)

        mod = _ada(c, w_ada[l], b_ada[l])
        mod8 = jnp.concatenate([mod.reshape(bsz, 6, d), jnp.zeros((bsz, 2, d), F32)], axis=1)

        q, k, v, z, xs, bm, cm, dta = _front(
            x, pos3, mod8, row(g_pre_mix[l]), wlat, row(g_q_lat[l]), row(g_kv_lat[l]), wq, wkv, wz, wxbc,
            conv_w[l].astype(F32), row(conv_b[l]), dtb, alog, inv128, tm=tm)
        attn = _attention(q, k, v, tq=tq)
        yg = _ssd(xs, bm, cm, dta, z, dsk, row(g_ssm_out[l]), n_sub=min(8, seq // CHUNK))
        x = _back(x, attn, yg, mod8, row(g_pre_mix[l]), wga, wgb, w_o_attn[l].astype(BF16),
                  w_o_ssm[l].astype(BF16), w_out[l].astype(BF16), row(g_post_mix[l]), row(g_pre_mlp[l]),
                  w_ff1[l].astype(BF16), w_ff2[l].astype(BF16), row(g_post_mlp[l]), tm=tm)
    return x
```

```python
import functools

import jax
import jax.numpy as jnp
from jax import lax
from jax.experimental import pallas as pl
from jax.experimental.pallas import tpu as pltpu

F32 = jnp.float32
BF16 = jnp.bfloat16
HIGHEST = lax.Precision.HIGHEST

N_HEADS = 8
D_NOPE = 128
D_ROPE = 64
D_V = 128
D_QK = D_NOPE + D_ROPE
R_Q = 256
R_KV = 256
ROPE_THETA = 10000.0
SSM_INNER = 2048
SSM_P = 64
SSM_HEADS = 32
SSM_GROUPS = 8
SSM_N = 128
CONV_K = 4
CHUNK = 128
EPS = 1e-6
LOG2E = 1.4426950408889634

VMEM_LIMIT_BYTES = 56 * 1024 * 1024
LANES = 128

_NT = (((1,), (1,)), ((), ()))


def _dot(a, b):
    return jnp.dot(a, b, preferred_element_type=F32)


def _dot_nt(a, b, precision=None):
    return lax.dot_general(a, b, _NT, precision=precision, preferred_element_type=F32)


def _rms(x, g):
    ms = jnp.mean(x * x, axis=-1, keepdims=True)
    return x * lax.rsqrt(ms + EPS) * g


def _sigmoid(x):
    return 1.0 / (1.0 + jnp.exp(-x))


def _const_spec(shape):
    n = len(shape)
    return pl.BlockSpec(shape, lambda *_: (0,) * n, pipeline_mode=pl.Buffered(1))


def _ada_kernel(c_ref, w_ref, b_ref, o_ref):
    c = c_ref[...]
    sc = c * _sigmoid(c)
    o_ref[...] = jnp.dot(sc, w_ref[...], precision=HIGHEST, preferred_element_type=F32) + b_ref[...]


def _ada(c, w_ada, b_ada):
    bsz, d = c.shape
    n = w_ada.shape[1]
    blk = 1024
    return pl.pallas_call(
        _ada_kernel,
        grid=(n // blk,),
        in_specs=[pl.BlockSpec((bsz, d), lambda j: (0, 0)),
                  pl.BlockSpec((d, blk), lambda j: (0, j)),
                  pl.BlockSpec((1, blk), lambda j: (0, j))],
        out_specs=pl.BlockSpec((bsz, blk), lambda j: (0, j)),
        out_shape=jax.ShapeDtypeStruct((bsz, n), F32),
        name="ada",
    )(c, w_ada, b_ada.reshape(1, n))


def _front_kernel(x_ref, pos_ref, mod_ref, gpre_ref, wlat_ref, gq_ref, gkv_ref, wq_ref, wkv_ref,
                  wz_ref, wxbc_ref, convw_ref, convb_ref, dtb_ref, alog_ref, inv_ref,
                  q_ref, k_ref, v_ref, z_ref, xs_ref, bm_ref, cm_ref, dta_ref,
                  carry_ref, cbuf_ref, tin_ref, tout_ref, *, tm, cw, qscale, pitch):
    i = pl.program_id(1)

    @pl.when(i == 0)
    def _():
        carry_ref[...] = jnp.zeros_like(carry_ref)

    x = x_ref[...]
    shift1 = mod_ref[0:1, :]
    scale1 = mod_ref[1:2, :]
    h = _rms(x, gpre_ref[...]) * (1.0 + scale1) + shift1
    hb = h.astype(BF16)

    lat = _dot(hb, wlat_ref[...])
    qn = _rms(lat[:, 0:R_Q], gq_ref[...]).astype(BF16)
    kvn = _rms(lat[:, R_Q:R_Q + R_KV], gkv_ref[...]).astype(BF16)
    kr = lat[:, 512:640]
    dtr = lat[:, 640:768]

    lane = lax.broadcasted_iota(jnp.int32, (1, LANES), 1)
    pre = dtr + dtb_ref[...]
    sp = jnp.maximum(pre, 0.0) + jnp.log1p(jnp.exp(-jnp.abs(pre)))
    arow = jnp.where(lane < 32, 1.0, jnp.where(lane < 64, -jnp.exp(alog_ref[...]), 0.0))
    dta_ref[...] = sp * arow

    ang = jnp.tile(inv_ref[...], (1, tm // LANES)) * pos_ref[...].astype(F32)
    cs = jnp.cos(ang)
    sn = jnp.sin(ang)
    rot = jnp.concatenate([cs, cs, -sn, sn], axis=0).T

    def rope(t):
        u = t * rot
        return u + pltpu.roll(u, 64, axis=1)

    kro = rope(kr)[:, 0:D_ROPE].astype(BF16)

    for hd in range(N_HEADS):
        qh = _dot(qn, wq_ref[hd])
        q_ref[hd, :, 0:D_NOPE] = (qh[:, 0:D_NOPE] * qscale).astype(BF16)
        q_ref[hd, :, D_NOPE:D_QK] = (rope(qh[:, 128:256])[:, 0:D_ROPE] * qscale).astype(BF16)
        kvh = _dot(kvn, wkv_ref[hd])
        k_ref[hd, :, 0:D_NOPE] = kvh[:, 0:D_NOPE].astype(BF16)
        k_ref[hd, :, D_NOPE:D_QK] = kro
        v_ref[hd] = kvh[:, 128:256].astype(BF16)

    d = x.shape[1]
    rows_p = 8 * pitch
    for j in range(d // LANES):
        for s in range(8):
            tin_ref[j, pl.ds(s, pitch, stride=8), :] = h[s * pitch:(s + 1) * pitch, j * LANES:(j + 1) * LANES]
    hp = jnp.concatenate([tin_ref[j] for j in range(d // LANES)], axis=1).astype(BF16)
    sub = lax.broadcasted_iota(jnp.int32, (8, cw), 0)
    n_xbc = wxbc_ref.shape[1]
    z_every = n_xbc // SSM_INNER
    for ci, c0 in enumerate(range(0, n_xbc, cw)):
        slot = ci % 2
        cols = slice(c0, c0 + cw)
        u = _dot(hp, wxbc_ref[:, cols])
        if ci % z_every == 0:
            zc = slice((ci // z_every) * cw, (ci // z_every + 1) * cw)
            z_ref[:, zc] = _dot(hb, wz_ref[:, zc]).astype(BF16)
        for j in range(1, CONV_K):
            grp = pltpu.roll(u[(pitch - j) * 8:(pitch - j + 1) * 8, :], 1, axis=0)
            prev = carry_ref[CONV_K - 1 - j:CONV_K - j, cols]
            cbuf_ref[slot, (CONV_K - 1 - j) * 8:(CONV_K - j) * 8, :] = jnp.where(sub == 0, prev, grp)
        hdr = (CONV_K - 1) * 8
        cbuf_ref[slot, hdr:hdr + rows_p, :] = u
        for j in range(1, CONV_K):
            t = tm - j
            r = (t % pitch) * 8 + t // pitch
            carry_ref[CONV_K - 1 - j:CONV_K - j, cols] = u[r:r + 1, :]
        acc = convb_ref[:, cols]
        for kk in range(CONV_K):
            acc = acc + cbuf_ref[slot, kk * 8:kk * 8 + rows_p, :] * convw_ref[kk:kk + 1, cols]
        for jj in range(cw // LANES):
            tout_ref[slot, jj] = acc[:, jj * LANES:(jj + 1) * LANES]
        v = jnp.concatenate(
            [jnp.concatenate([tout_ref[slot, jj, pl.ds(s, pitch, stride=8), :] for s in range(8)], axis=0)
             for jj in range(cw // LANES)], axis=1)
        sv = (v * _sigmoid(v)).astype(BF16)
        if c0 < SSM_INNER:
            xs_ref[:, cols] = sv
        elif c0 < SSM_INNER + SSM_GROUPS * SSM_N:
            o0 = c0 - SSM_INNER
            bm_ref[:, o0:o0 + cw] = sv
        else:
            o0 = c0 - SSM_INNER - SSM_GROUPS * SSM_N
            cm_ref[:, o0:o0 + cw] = sv


def _front(x, pos3, mod8, gpre, wlat, gq, gkv, wq, wkv, wz, wxbc, convw, convb, dtb, alog, inv128, *, tm):
    bsz, seq, d = x.shape
    cw = 256
    assert tm % 64 == 0
    pitch = tm // 8
    qscale = float(D_QK ** -0.5 * LOG2E)
    n_bc = SSM_GROUPS * SSM_N
    tok = lambda width: pl.BlockSpec((None, tm, width), lambda b, i: (b, i, 0))
    head = lambda width: pl.BlockSpec((None, N_HEADS, tm, width), lambda b, i: (b, 0, i, 0))
    in_specs = [
        tok(d),
        pl.BlockSpec((None, 1, tm), lambda b, i: (b, 0, i)),
        pl.BlockSpec((None, 8, d), lambda b, i: (b, 0, 0)),
        _const_spec(gpre.shape), _const_spec(wlat.shape), _const_spec(gq.shape), _const_spec(gkv.shape),
        _const_spec(wq.shape), _const_spec(wkv.shape), _const_spec(wz.shape), _const_spec(wxbc.shape),
        _const_spec(convw.shape), _const_spec(convb.shape), _const_spec(dtb.shape), _const_spec(alog.shape),
        _const_spec(inv128.shape),
    ]
    out_specs = [head(D_QK), head(D_QK), head(D_V), tok(SSM_INNER), tok(SSM_INNER), tok(n_bc), tok(n_bc),
                 tok(LANES)]
    out_shape = [
        jax.ShapeDtypeStruct((bsz, N_HEADS, seq, D_QK), BF16),
        jax.ShapeDtypeStruct((bsz, N_HEADS, seq, D_QK), BF16),
        jax.ShapeDtypeStruct((bsz, N_HEADS, seq, D_V), BF16),
        jax.ShapeDtypeStruct((bsz, seq, SSM_INNER), BF16),
        jax.ShapeDtypeStruct((bsz, seq, SSM_INNER), BF16),
        jax.ShapeDtypeStruct((bsz, seq, n_bc), BF16),
        jax.ShapeDtypeStruct((bsz, seq, n_bc), BF16),
        jax.ShapeDtypeStruct((bsz, seq, LANES), F32),
    ]
    return pl.pallas_call(
        functools.partial(_front_kernel, tm=tm, cw=cw, qscale=qscale, pitch=pitch),
        grid=(bsz, seq // tm),
        in_specs=in_specs,
        out_specs=out_specs,
        out_shape=out_shape,
        scratch_shapes=[pltpu.VMEM((8, wxbc.shape[1]), F32),
                        pltpu.VMEM((2, (CONV_K - 1) * 8 + 8 * pitch, cw), F32),
                        pltpu.VMEM((d // LANES, 8 * pitch, LANES), F32),
                        pltpu.VMEM((2, cw // LANES, 8 * pitch, LANES), F32)],
        compiler_params=pltpu.CompilerParams(dimension_semantics=("parallel", "arbitrary"),
                                             vmem_limit_bytes=VMEM_LIMIT_BYTES),
        name="front",
    )(x, pos3, mod8, gpre, wlat, gq, gkv, wq, wkv, wz, wxbc, convw, convb, dtb, alog, inv128)


def _attn_kernel(q_ref, k_ref, v_ref, o_ref, sa_ref, sb_ref, m_ref, l_ref, acc_ref, *, tq, tk, n_tiles):
    bufs = (sa_ref, sb_ref)
    row = lax.broadcasted_iota(jnp.int32, (tq, tk), 0)
    col = lax.broadcasted_iota(jnp.int32, (tq, tk), 1)
    row_b = lax.broadcasted_iota(jnp.int32, (tk, tk), 0)
    col_b = lax.broadcasted_iota(jnp.int32, (tk, tk), 1)

    def scores(r0, n_rows, k0, kw):
        return _dot_nt(q_ref[r0:r0 + n_rows, :], k_ref[k0:k0 + kw, :])

    def update(par, rows, k0, s, first=False):
        n, kw = s.shape
        s_max = jnp.max(s, axis=-1, keepdims=True)
        if first:
            m_new = jnp.broadcast_to(s_max, (n, LANES))
            alpha = None
        else:
            m_prev = m_ref[par, rows, :]
            m_new = jnp.maximum(m_prev, s_max)
            alpha = jnp.exp2(m_prev - m_new)
        p = jnp.exp2(s - jnp.tile(m_new, (1, kw // LANES)))
        p_sum = jnp.sum(p, axis=-1, keepdims=True)
        m_ref[par, rows, :] = m_new
        l_ref[par, rows, :] = jnp.broadcast_to(p_sum, (n, LANES)) if first else alpha * l_ref[par, rows, :] + p_sum
        pv = _dot(p.astype(BF16), v_ref[k0:k0 + kw, :])
        return alpha, pv

    def accumulate(par, rows, alpha, pv):
        acc_ref[par, rows, :] = pv if alpha is None else alpha * acc_ref[par, rows, :] + pv

    allrows = slice(0, tq)
    bot = slice(tk, tq)
    cur = 0
    bufs[cur][:, 0:tk] = scores(0, tq, 0, tk)
    for i in range(n_tiles):
        par = i % 2
        r0 = i * tq
        for w in range(i):
            alpha, pv = update(par, allrows, w * tq, bufs[cur][...], first=(w == 0))
            if w + 1 < i:
                bufs[1 - cur][...] = scores(r0, tq, (w + 1) * tq, tq)
            else:
                bufs[1 - cur][:, 0:tk] = scores(r0, tq, r0, tk)
            accumulate(par, allrows, alpha, pv)
            cur = 1 - cur
        alpha, pv = update(par, allrows, r0, jnp.where(col <= row, bufs[cur][:, 0:tk], -jnp.inf), first=(i == 0))
        bufs[1 - cur][0:tk, 0:tk] = scores(r0 + tk, tk, r0 + tk, tk)
        accumulate(par, allrows, alpha, pv)
        alpha, pv = update(par, bot, r0 + tk, jnp.where(col_b <= row_b, bufs[1 - cur][0:tk, 0:tk], -jnp.inf))
        if i + 1 < n_tiles:
            bufs[cur][...] = scores(r0 + tq, tq, 0, tq)
        accumulate(par, bot, alpha, pv)
        o_ref[r0:r0 + tq, :] = (acc_ref[par] / l_ref[par]).astype(BF16)


def _attention(q, k, v, *, tq):
    bsz, nh, seq, _ = q.shape
    whole = lambda width: pl.BlockSpec((None, None, seq, width), lambda b, h: (b, h, 0, 0))
    return pl.pallas_call(
        functools.partial(_attn_kernel, tq=tq, tk=tq // 2, n_tiles=seq // tq),
        grid=(bsz, nh),
        in_specs=[whole(D_QK), whole(D_QK), whole(D_V)],
        out_specs=pl.BlockSpec((None, seq, D_V), lambda b, h: (b, 0, h)),
        out_shape=jax.ShapeDtypeStruct((bsz, seq, nh * D_V), BF16),
        scratch_shapes=[pltpu.VMEM((tq, tq), F32), pltpu.VMEM((tq, tq), F32),
                        pltpu.VMEM((2, tq, LANES), F32), pltpu.VMEM((2, tq, LANES), F32),
                        pltpu.VMEM((2, tq, D_V), F32)],
        compiler_params=pltpu.CompilerParams(dimension_semantics=("parallel", "parallel"),
                                             vmem_limit_bytes=VMEM_LIMIT_BYTES),
        name="attn",
    )(q, k, v)


def _ssd_kernel(xs_ref, bm_ref, cm_ref, dta_ref, z_ref, dsk_ref, gss_ref, o_ref,
                state_ref, rows_ref, *, n_sub):
    ci = pl.program_id(1)

    @pl.when(ci == 0)
    def _():
        state_ref[...] = jnp.zeros_like(state_ref)

    L = CHUNK
    r_i = lax.broadcasted_iota(jnp.int32, (L, L), 0)
    c_i = lax.broadcasted_iota(jnp.int32, (L, L), 1)
    tril = c_i <= r_i
    tril_f = tril.astype(F32)
    eye_b = (r_i == c_i).astype(BF16)
    lane = lax.broadcasted_iota(jnp.int32, (1, L), 1)
    in_a = (lane >= 32) & (lane < 64)
    first = lane < 64

    for sc in range(n_sub):
        rows = slice(sc * L, (sc + 1) * L)
        dta = dta_ref[rows, :]
        cum = jnp.dot(tril_f, dta, precision=HIGHEST, preferred_element_type=F32)
        acm2 = jnp.where(in_a, cum * LOG2E, 0.0)
        comb_t = jnp.where(in_a, acm2, dta).T
        dt_t = comb_t[0:SSM_HEADS]
        a2_t = comb_t[SSM_HEADS:2 * SSM_HEADS]
        rows_ref[sc, 0] = a2_t - jnp.log2(dt_t)
        rows_ref[sc, 1] = jnp.exp2(a2_t[:, L - 1:L] - a2_t) * dt_t
        cd_all = jnp.exp2(acm2[L - 1:L, :])

        cgs = [cm_ref[rows, g * SSM_N:(g + 1) * SSM_N] for g in range(SSM_GROUPS)]
        bgs = [bm_ref[rows, g * SSM_N:(g + 1) * SSM_N] for g in range(SSM_GROUPS)]
        cbs = [_dot_nt(cgs[g], bgs[g]) for g in range(SSM_GROUPS)]
        bts = [_dot_nt(eye_b, bgs[g]) for g in range(SSM_GROUPS)]

        def finish(g, parts):
            ys = []
            for pr, (yy, ss, s_old) in enumerate(parts):
                hp = g * 2 + pr
                ys.append(jnp.where(first, yy[0:L], yy[L:2 * L]))
                h0 = hp * 2
                cdrow = jnp.where(first, cd_all[:, 32 + h0:33 + h0], cd_all[:, 33 + h0:34 + h0])
                state_ref[hp] = cdrow * s_old + jnp.where(first, ss[0:SSM_N], ss[SSM_N:2 * SSM_N])
            sl = slice(g * 256, (g + 1) * 256)
            xg = xs_ref[rows, sl].astype(F32)
            y = jnp.concatenate(ys, axis=1) + dsk_ref[:, sl] * xg
            zg = z_ref[rows, sl].astype(F32)
            yz = y * (zg * _sigmoid(zg))
            o_ref[rows, sl] = _rms(yz, gss_ref[:, sl]).astype(BF16)

        pending = None
        for g in range(SSM_GROUPS):
            cg32 = cgs[g].astype(F32)
            parts = []
            for pr in range(2):
                hp = g * 2 + pr
                xpair = xs_ref[rows, hp * 128:(hp + 1) * 128]
                s_old = state_ref[hp]
                rhs = jnp.concatenate([xpair, s_old.astype(BF16)], axis=0)
                lhs_parts = []
                bw_parts = []
                for e in range(2):
                    hh = hp * 2 + e
                    colb = jnp.broadcast_to(acm2[:, 32 + hh:33 + hh], (L, L))
                    dec = jnp.exp2(jnp.where(tril, colb - rows_ref[sc, 0, hh:hh + 1, :], -jnp.inf))
                    mh = (cbs[g] * dec).astype(BF16)
                    ce = (cg32 * jnp.exp2(colb)).astype(BF16)
                    lhs_parts.append(jnp.concatenate([mh, ce], axis=1))
                    bw_parts.append((bts[g] * rows_ref[sc, 1, hh:hh + 1, :]).astype(BF16))
                yy = _dot(jnp.concatenate(lhs_parts, axis=0), rhs)
                ss = _dot(jnp.concatenate(bw_parts, axis=0), xpair)
                parts.append((yy, ss, s_old))
            if pending is not None:
                finish(*pending)
            pending = (g, parts)
        finish(*pending)


def _ssd(xs, bm, cm, dta, z, dsk, gss, *, n_sub):
    bsz, seq, _ = xs.shape
    n_bc = SSM_GROUPS * SSM_N
    rows = n_sub * CHUNK
    tok = lambda width: pl.BlockSpec((None, rows, width), lambda b, c: (b, c, 0))
    return pl.pallas_call(
        functools.partial(_ssd_kernel, n_sub=n_sub),
        grid=(bsz, seq // rows),
        in_specs=[tok(SSM_INNER), tok(n_bc), tok(n_bc), tok(LANES), tok(SSM_INNER),
                  _const_spec(dsk.shape), _const_spec(gss.shape)],
        out_specs=tok(SSM_INNER),
        out_shape=jax.ShapeDtypeStruct((bsz, seq, SSM_INNER), BF16),
        scratch_shapes=[pltpu.VMEM((SSM_HEADS // 2, SSM_N, 128), F32),
                        pltpu.VMEM((n_sub, 2, SSM_HEADS, CHUNK), F32)],
        compiler_params=pltpu.CompilerParams(dimension_semantics=("parallel", "arbitrary"),
                                             vmem_limit_bytes=VMEM_LIMIT_BYTES),
        name="ssd",
    )(xs, bm, cm, dta, z, dsk, gss)


def _back_kernel(x_ref, attn_ref, yg_ref, mod_ref, gpre1_ref, wga_ref, wgb_ref, woa_ref, wob_ref, wout_ref,
                 gpost1_ref, gpre2_ref, w1_ref, w2_ref, gpost2_ref, o_ref, mb_ref, acc_ref, *, cw, cw_ff):
    x = x_ref[...]
    shift1, scale1, gate1 = mod_ref[0:1, :], mod_ref[1:2, :], mod_ref[2:3, :]
    shift2, scale2, gate2 = mod_ref[3:4, :], mod_ref[4:5, :], mod_ref[5:6, :]
    hb = (_rms(x, gpre1_ref[...]) * (1.0 + scale1) + shift1).astype(BF16)
    at = attn_ref[...]
    yg = yg_ref[...]
    d = x.shape[1]
    for c0 in range(0, d, cw):
        sl = slice(c0, c0 + cw)
        ga = _sigmoid(_dot(hb, wga_ref[:, sl]))
        gb = _sigmoid(_dot(hb, wgb_ref[:, sl]))
        merged = ga * _dot(at, woa_ref[:, sl]) + gb * _dot(yg, wob_ref[:, sl])
        mb_ref[:, sl] = merged.astype(BF16)
    mix = _dot(mb_ref[...], wout_ref[...])
    o_ref[...] = x + gate1 * _rms(mix, gpost1_ref[...])
    hb2 = (_rms(o_ref[...], gpre2_ref[...]) * (1.0 + scale2) + shift2).astype(BF16)
    dff = w1_ref.shape[1]
    for ci, c0 in enumerate(range(0, dff, cw_ff)):
        a = jnp.maximum(_dot(hb2, w1_ref[:, c0:c0 + cw_ff]), 0.0)
        part = _dot((a * a).astype(BF16), w2_ref[c0:c0 + cw_ff, :])
        if ci == 0:
            acc_ref[...] = part
        else:
            acc_ref[...] += part
    o_ref[...] = o_ref[...] + gate2 * _rms(acc_ref[...], gpost2_ref[...])


def _back(x, attn, yg, mod8, gpre1, wga, wgb, woa, wob, wout, gpost1, gpre2, w1, w2, gpost2, *, tm):
    bsz, seq, d = x.shape
    tok = lambda width: pl.BlockSpec((None, tm, width), lambda b, i: (b, i, 0))
    consts = (gpre1, wga, wgb, woa, wob, wout, gpost1, gpre2, w1, w2, gpost2)
    return pl.pallas_call(
        functools.partial(_back_kernel, cw=512, cw_ff=1024),
        grid=(bsz, seq // tm),
        in_specs=[tok(d), tok(attn.shape[2]), tok(yg.shape[2]),
                  pl.BlockSpec((None, 8, d), lambda b, i: (b, 0, 0))] + [_const_spec(c.shape) for c in consts],
        out_specs=tok(d),
        out_shape=jax.ShapeDtypeStruct((bsz, seq, d), F32),
        scratch_shapes=[pltpu.VMEM((tm, d), BF16), pltpu.VMEM((tm, d), F32)],
        compiler_params=pltpu.CompilerParams(dimension_semantics=("parallel", "parallel"),
                                             vmem_limit_bytes=VMEM_LIMIT_BYTES),
        name="back",
    )(x, attn, yg, mod8, *consts)


def _pad_row(v, width, offset=0):
    out = jnp.zeros((1, width), F32)
    return out.at[0, offset:offset + v.shape[0]].set(v.astype(F32))


def kernel(x, c, positions, w_ada, b_ada, g_pre_mix, g_post_mix, w_in, g_q_lat, g_kv_lat, w_uq, w_ukv,
           w_o_attn, conv_w, conv_b, dt_bias, a_log, d_skip, g_ssm_out, w_o_ssm, w_out, g_pre_mlp,
           g_post_mlp, w_ff1, w_ff2):
    bsz, seq, d = x.shape
    depth = w_ada.shape[0]
    tm = min(512, seq)
    tq = min(1024, seq)
    half = D_ROPE // 2
    inv = ROPE_THETA ** (-jnp.arange(half, dtype=F32) / half)
    inv128 = jnp.broadcast_to(inv[:, None], (half, LANES))
    pos3 = positions.reshape(bsz, 1, seq)
    row = lambda v: v.reshape(1, -1).astype(F32)

    for l in range(depth):
        wi = w_in[l]
        o_q, o_kv, o_kr, o_z = 0, R_Q, R_Q + R_KV, R_Q + R_KV + D_ROPE
        o_xbc = o_z + SSM_INNER
        o_dt = o_xbc + SSM_INNER + 2 * SSM_GROUPS * SSM_N
        o_ga = o_dt + SSM_HEADS
        o_gb = o_ga + d
        kr1 = wi[:, o_kr:o_kr + half]
        kr2 = wi[:, o_kr + half:o_kr + D_ROPE]
        wdt = wi[:, o_dt:o_dt + SSM_HEADS]
        wlat = jnp.concatenate(
            [wi[:, o_q:o_kr], kr1, kr2, kr2, kr1, wdt, wdt, jnp.zeros((d, 64), F32)], axis=1).astype(BF16)
        wz = wi[:, o_z:o_xbc].astype(BF16)
        wxbc = wi[:, o_xbc:o_dt].astype(BF16)
        wga = wi[:, o_ga:o_gb].astype(BF16)
        wgb = wi[:, o_gb:o_gb + d].astype(BF16)
        wq3 = w_uq[l].reshape(R_Q, N_HEADS, D_QK)
        q1 = wq3[:, :, D_NOPE:D_NOPE + half]
        q2 = wq3[:, :, D_NOPE + half:]
        wq = jnp.concatenate([wq3[:, :, :D_NOPE], q1, q2, q2, q1], axis=2).transpose(1, 0, 2).astype(BF16)
        wkv = w_ukv[l].reshape(R_KV, N_HEADS, D_NOPE + D_V).transpose(1, 0, 2).astype(BF16)
        dtb = jnp.concatenate([row(dt_bias[l]), row(dt_bias[l]), jnp.zeros((1, 64), F32)], axis=1)
        alog = _pad_row(a_log[l], LANES, 32)
        dsk = jnp.repeat(d_skip[l].astype(F32), SSM_P).reshape(1, SSM_INNER)

        mod = _ada(c, w_ada[l], b_ada[l])
        mod8 = jnp.concatenate([mod.reshape(bsz, 6, d), jnp.zeros((bsz, 2, d), F32)], axis=1)

        q, k, v, z, xs, bm, cm, dta = _front(
            x, pos3, mod8, row(g_pre_mix[l]), wlat, row(g_q_lat[l]), row(g_kv_lat[l]), wq, wkv, wz, wxbc,
            conv_w[l].astype(F32), row(conv_b[l]), dtb, alog, inv128, tm=tm)
        attn = _attention(q, k, v, tq=tq)
        yg = _ssd(xs, bm, cm, dta, z, dsk, row(g_ssm_out[l]), n_sub=min(8, seq // CHUNK))
        x = _back(x, attn, yg, mod8, row(g_pre_mix[l]), wga, wgb, w_o_attn[l].astype(BF16),
                  w_o_ssm[l].astype(BF16), w_out[l].astype(BF16), row(g_post_mix[l]), row(g_pre_mlp[l]),
                  w_ff1[l].astype(BF16), w_ff2[l].astype(BF16), row(g_post_mlp[l]), tm=tm)
    return x
```
